```python
import math
import jax
import jax.numpy as jnp
from jax import lax
import numpy as np

D_MODEL = 1024
BATCH = 4
SEQ = 4096
DEPTH = 4
DEC_BATCH = 32
DEC_SEQ = 1
PAST_LEN = 8192
PAGE_SIZE = 128

S5_CH_PER_GROUP = 16
S5_WIDTH = D_MODEL // 2
S5_GROUPS = S5_WIDTH // S5_CH_PER_GROUP
S5_STATE = 64
SB_HEAD_DIM = 64
SB_WIDTH = D_MODEL // 2
SB_HEADS = SB_WIDTH // SB_HEAD_DIM
SB_BLOCK = 128
SB_BIAS_INIT = -6.0
M2_HEAD_DIM = 64
M2_INNER = D_MODEL // 2
M2_HEADS = M2_INNER // M2_HEAD_DIM
M2_GROUPS = 2
M2_STATE = 128
M2_CONV = 4
M2_CONV_DIM = M2_INNER + 2 * M2_GROUPS * M2_STATE
M2_CHUNK = 128
N_BRANCH = 3
IN_SIZES = (S5_WIDTH, SB_WIDTH, SB_WIDTH, SB_WIDTH, M2_INNER, M2_CONV_DIM, M2_HEADS, D_MODEL, D_MODEL, D_MODEL)
IN_PROJ = S5_WIDTH + 3 * SB_WIDTH + M2_INNER + M2_CONV_DIM + M2_HEADS + N_BRANCH * D_MODEL
MEM_TOKENS = 256
MEM_HEADS = 4
MEM_HEAD_DIM = D_MODEL // MEM_HEADS
D_FF = 4 * D_MODEL
EPS = 1e-6

kernel_name = 'hybrid_s5_stickbreak_ssd_step'


def rmsnorm(x, g):
    xf = x.astype(jnp.float32)
    y = xf * lax.rsqrt(jnp.mean(xf * xf, axis=-1, keepdims=True) + EPS)
    return (y * g.astype(jnp.float32)).astype(x.dtype)


def split_in(z):
    cuts, acc = [], 0
    for s in IN_SIZES[:-1]:
        acc += s
        cuts.append(acc)
    return jnp.split(z, cuts, axis=-1)


def s5_branch(u, h0_re, h0_im, lam_re, lam_im, log_dt, b_re, b_im, c_re, c_im, d, w_glu, b_glu):
    f32 = jnp.float32
    bsz, L, _ = u.shape
    uf = u.astype(f32)
    ug = uf.reshape(bsz, L, S5_GROUPS, S5_CH_PER_GROUP)
    lam_re = lam_re.astype(f32)
    lam_im = lam_im.astype(f32)
    dt = jnp.exp(log_dt.astype(f32))[:, None]
    mag = jnp.exp(lam_re * dt)
    lb_re = mag * jnp.cos(lam_im * dt)
    lb_im = mag * jnp.sin(lam_im * dt)
    den = lam_re * lam_re + lam_im * lam_im
    f_re = ((lb_re - 1.0) * lam_re + lb_im * lam_im) / den
    f_im = (lb_im * lam_re - (lb_re - 1.0) * lam_im) / den
    b_re = b_re.astype(f32)
    b_im = b_im.astype(f32)
    bb_re = f_re[..., None] * b_re - f_im[..., None] * b_im
    bb_im = f_re[..., None] * b_im + f_im[..., None] * b_re
    bu_re = jnp.einsum('blgc,gnc->blgn', ug, bb_re)
    bu_im = jnp.einsum('blgc,gnc->blgn', ug, bb_im)
    a_re = jnp.broadcast_to(lb_re, bu_re.shape)
    a_im = jnp.broadcast_to(lb_im, bu_im.shape)

    def combine(e1, e2):
        a1r, a1i, b1r, b1i = e1
        a2r, a2i, b2r, b2i = e2
        return (a2r * a1r - a2i * a1i, a2r * a1i + a2i * a1r,
                a2r * b1r - a2i * b1i + b2r, a2r * b1i + a2i * b1r + b2i)

    ar, ai, hr, hi = lax.associative_scan(combine, (a_re, a_im, bu_re, bu_im), axis=1)
    h0r = h0_re.astype(f32)[:, None]
    h0i = h0_im.astype(f32)[:, None]
    hr_full = hr + ar * h0r - ai * h0i
    hi_full = hi + ar * h0i + ai * h0r
    y = (jnp.einsum('blgn,gcn->blgc', hr_full, c_re.astype(f32))
         - jnp.einsum('blgn,gcn->blgc', hi_full, c_im.astype(f32)))
    y = y.reshape(bsz, L, S5_WIDTH) + d.astype(f32) * uf
    g = jax.nn.gelu(y)
    out = g * jax.nn.sigmoid(g @ w_glu.astype(f32) + b_glu.astype(f32))
    return out.astype(u.dtype), hr_full[:, -1], hi_full[:, -1]


def sb_attend(q, k, v, q_pos, k_pos, bias):
    f32 = jnp.float32
    z = (jnp.einsum('bqhd,bthd->bhqt', q.astype(f32), k.astype(f32)) * (SB_HEAD_DIM ** -0.5)
         + bias.astype(f32)[None, :, None, None])
    mask = k_pos[None, :] < q_pos[:, None]
    log_keep = jnp.where(mask, jax.nn.log_sigmoid(-z), 0.0)
    between = lax.cumsum(log_keep, axis=3, reverse=True) - log_keep
    w = jnp.where(mask, jnp.exp(jax.nn.log_sigmoid(z) + between), 0.0)
    return jnp.einsum('bhqt,bthd->bqhd', w, v.astype(f32)).astype(v.dtype)


def sb_prompt(q, k, v, bias):
    bsz, L = q.shape[:2]
    nb = L // SB_BLOCK
    qb = q.reshape(bsz, nb, SB_BLOCK, SB_HEADS, SB_HEAD_DIM).transpose(1, 0, 2, 3, 4)
    starts = jnp.arange(nb, dtype=jnp.int32) * SB_BLOCK
    k_pos = jnp.arange(L, dtype=jnp.int32)

    def one_block(args):
        qi, s0 = args
        return sb_attend(qi, k, v, s0 + jnp.arange(SB_BLOCK, dtype=jnp.int32), k_pos, bias)

    out = lax.map(one_block, (qb, starts))
    return out.transpose(1, 0, 2, 3, 4).reshape(bsz, L, SB_HEADS, SB_HEAD_DIM)


def causal_conv(u, buf, w, b):
    L = u.shape[1]
    full = jnp.concatenate([buf.astype(u.dtype), u], axis=1)
    out = b
    for j in range(M2_CONV):
        out = out + full[:, j:j + L] * w[j]
    return out, full[:, L:]


def ssd_chunked(xs, dt, a, bm, cm):
    f32 = jnp.float32
    bsz, L = xs.shape[:2]
    nc = L // M2_CHUNK
    hg = M2_HEADS // M2_GROUPS
    x = xs.astype(f32).reshape(bsz, nc, M2_CHUNK, M2_GROUPS, hg, M2_HEAD_DIM)
    dt = dt.reshape(bsz, nc, M2_CHUNK, M2_GROUPS, hg)
    bm = bm.astype(f32).reshape(bsz, nc, M2_CHUNK, M2_GROUPS, M2_STATE)
    cm = cm.astype(f32).reshape(bsz, nc, M2_CHUNK, M2_GROUPS, M2_STATE)
    acum = jnp.cumsum(dt * a.reshape(M2_GROUPS, hg), axis=2)
    causal = jnp.tril(jnp.ones((M2_CHUNK, M2_CHUNK), dtype=bool))[:, :, None, None]
    seg = acum[:, :, :, None] - acum[:, :, None, :]
    decay = jnp.exp(jnp.where(causal, seg, -jnp.inf))
    cb = jnp.einsum('bcqgn,bcsgn->bcqsg', cm, bm)
    mix = cb[..., None] * decay * dt[:, :, None]
    y_diag = jnp.einsum('bcqsgh,bcsghp->bcqghp', mix, x)
    w_end = jnp.exp(acum[:, :, -1:] - acum) * dt
    states = jnp.einsum('bcsgn,bcsghp->bcghpn', bm, w_end[..., None] * x)
    chunk_decay = jnp.exp(acum[:, :, -1])

    def step(h, inp):
        s_c, d_c = inp
        return d_c[..., None, None] * h + s_c, h

    h0 = jnp.zeros((bsz, M2_GROUPS, hg, M2_HEAD_DIM, M2_STATE), f32)
    h_last, h_prev = lax.scan(step, h0, (jnp.moveaxis(states, 1, 0), jnp.moveaxis(chunk_decay, 1, 0)))
    h_prev = jnp.moveaxis(h_prev, 0, 1)
    y_off = jnp.einsum('bcqgn,bcghpn->bcqghp', cm, h_prev) * jnp.exp(acum)[..., None]
    y = (y_diag + y_off).reshape(bsz, L, M2_HEADS, M2_HEAD_DIM)
    return y, h_last.reshape(bsz, M2_HEADS, M2_HEAD_DIM, M2_STATE)


def ssd_recurrent(xs, dt, a, bm, cm, h0):
    f32 = jnp.float32
    hg = M2_HEADS // M2_GROUPS
    bh = jnp.repeat(bm.astype(f32), hg, axis=2)
    ch = jnp.repeat(cm.astype(f32), hg, axis=2)

    def step(h, inp):
        x_t, dt_t, b_t, c_t = inp
        h = (jnp.exp(dt_t * a)[:, :, None, None] * h
             + (dt_t[:, :, None] * x_t)[..., None] * b_t[:, :, None, :])
        return h, jnp.einsum('bhpn,bhn->bhp', h, c_t)

    seq = (jnp.moveaxis(xs.astype(f32), 1, 0), jnp.moveaxis(dt, 1, 0),
           jnp.moveaxis(bh, 1, 0), jnp.moveaxis(ch, 1, 0))
    h_last, ys = lax.scan(step, h0.astype(f32), seq)
    return jnp.moveaxis(ys, 0, 1), h_last


def mixing_sublayer(h, prm, s5_h0_re, s5_h0_im, conv_buf, sb_fn, ssd_fn):
    (g_norm, w_in, s5p, sb_bias, conv_w, conv_b, dt_bias, a, m2_d, m2_norm,
     w_br_s5, w_br_sb, w_br_m2, w_out) = prm
    f32 = jnp.float32
    bsz, L, _ = h.shape
    z = rmsnorm(h, g_norm) @ w_in
    u, q, k, v, zg, xbc, dt_raw, g_s5, g_sb, g_m2 = split_in(z)
    y_s5, s5_re, s5_im = s5_branch(u, s5_h0_re, s5_h0_im, *s5p)
    q = q.reshape(bsz, L, SB_HEADS, SB_HEAD_DIM)
    k = k.reshape(bsz, L, SB_HEADS, SB_HEAD_DIM)
    v = v.reshape(bsz, L, SB_HEADS, SB_HEAD_DIM)
    y_sb = sb_fn(q, k, v, sb_bias).reshape(bsz, L, SB_WIDTH)
    xbc, new_buf = causal_conv(xbc, conv_buf, conv_w, conv_b)
    xbc = jax.nn.silu(xbc)
    nb = M2_GROUPS * M2_STATE
    xs = xbc[..., :M2_INNER].reshape(bsz, L, M2_HEADS, M2_HEAD_DIM)
    bm = xbc[..., M2_INNER:M2_INNER + nb].reshape(bsz, L, M2_GROUPS, M2_STATE)
    cm = xbc[..., M2_INNER + nb:].reshape(bsz, L, M2_GROUPS, M2_STATE)
    dt = jax.nn.softplus(dt_raw.astype(f32) + dt_bias.astype(f32))
    y_m2, ssm = ssd_fn(xs, dt, a, bm, cm)
    y_m2 = y_m2 + m2_d.astype(f32)[:, None] * xs.astype(f32)
    y_m2 = rmsnorm(y_m2.reshape(bsz, L, M2_INNER) * jax.nn.silu(zg.astype(f32)), m2_norm).astype(h.dtype)
    merged = (jax.nn.sigmoid(g_s5) * (y_s5 @ w_br_s5)
              + jax.nn.sigmoid(g_sb) * (y_sb @ w_br_sb)
              + jax.nn.sigmoid(g_m2) * (y_m2 @ w_br_m2))
    return h + merged @ w_out, k, v, s5_re, s5_im, ssm, new_buf


def mem_kv(mem, wk, wv):
    bsz, m, _ = mem.shape
    return ((mem @ wk).reshape(bsz, m, MEM_HEADS, MEM_HEAD_DIM),
            (mem @ wv).reshape(bsz, m, MEM_HEADS, MEM_HEAD_DIM))


def cross_attn(xn, k, v, wq, wo):
    bsz, L, _ = xn.shape
    q = (xn @ wq).reshape(bsz, L, MEM_HEADS, MEM_HEAD_DIM)
    s = jnp.einsum('blhd,bmhd->bhlm', q.astype(jnp.float32), k.astype(jnp.float32)) * (MEM_HEAD_DIM ** -0.5)
    p = jax.nn.softmax(s, axis=-1)
    o = jnp.einsum('bhlm,bmhd->blhd', p, v.astype(jnp.float32)).astype(xn.dtype)
    return o.reshape(bsz, L, D_MODEL) @ wo


def sqrelu_mlp(x, w1, w2):
    return jnp.square(jax.nn.relu(x @ w1)) @ w2


def setup_inputs(seed: int = 0) -> dict:
    key = jax.random.key(seed)
    ks = iter(jax.random.split(key, 64))
    f32 = jnp.float32

    def nrm(shape, scale=1.0):
        return scale * jax.random.normal(next(ks), shape, f32)

    def gain(shape):
        return 1.0 + 0.02 * jax.random.normal(next(ks), shape, f32)

    n_pages = PAST_LEN // PAGE_SIZE
    n_pool = (DEC_BATCH * n_pages * 5) // 4
    page_table = jax.random.permutation(next(ks), n_pool)[:DEC_BATCH * n_pages].reshape(DEC_BATCH, n_pages).astype(jnp.int32)
    L = DEPTH
    s5_log_dt = jax.random.uniform(next(ks), (L, S5_GROUPS), f32, math.log(1e-3), math.log(1e-1))
    m2_dt = jnp.exp(jax.random.uniform(next(ks), (L, M2_HEADS), f32, math.log(1e-3), math.log(1e-1)))
    m2_dt_bias = m2_dt + jnp.log(-jnp.expm1(-m2_dt))
    m2_a_log = jnp.log(jax.random.uniform(next(ks), (L, M2_HEADS), f32, 1.0, 16.0))
    lam_im = jnp.pi * jnp.arange(S5_STATE, dtype=f32) + nrm((L, S5_GROUPS, S5_STATE), 0.01)
    return {
        'x_prompt': nrm((BATCH, SEQ, D_MODEL)),
        'x_sample': nrm((DEC_BATCH, DEC_SEQ, D_MODEL)),
        'mem_prompt': nrm((BATCH, MEM_TOKENS, D_MODEL)),
        'cache_sb_k': nrm((L, n_pool, PAGE_SIZE, SB_HEADS, SB_HEAD_DIM)),
        'cache_sb_v': nrm((L, n_pool, PAGE_SIZE, SB_HEADS, SB_HEAD_DIM)),
        'state_s5_re': nrm((L, DEC_BATCH, S5_GROUPS, S5_STATE), 0.1),
        'state_s5_im': nrm((L, DEC_BATCH, S5_GROUPS, S5_STATE), 0.1),
        'state_ssm': nrm((L, DEC_BATCH, M2_HEADS, M2_HEAD_DIM, M2_STATE), 0.1),
        'state_conv': nrm((L, DEC_BATCH, M2_CONV - 1, M2_CONV_DIM)),
        'cache_mem_k': nrm((L, DEC_BATCH, MEM_TOKENS, MEM_HEADS, MEM_HEAD_DIM)),
        'cache_mem_v': nrm((L, DEC_BATCH, MEM_TOKENS, MEM_HEADS, MEM_HEAD_DIM)),
        'page_table': page_table,
        'norm_mix': gain((L, D_MODEL)),
        'w_in': nrm((L, D_MODEL, IN_PROJ), D_MODEL ** -0.5),
        's5_lambda_re': -0.5 + nrm((L, S5_GROUPS, S5_STATE), 0.01),
        's5_lambda_im': lam_im,
        's5_log_dt': s5_log_dt,
        's5_b_re': nrm((L, S5_GROUPS, S5_STATE, S5_CH_PER_GROUP), S5_CH_PER_GROUP ** -0.5),
        's5_b_im': nrm((L, S5_GROUPS, S5_STATE, S5_CH_PER_GROUP), S5_CH_PER_GROUP ** -0.5),
        's5_c_re': nrm((L, S5_GROUPS, S5_CH_PER_GROUP, S5_STATE), S5_STATE ** -0.5),
        's5_c_im': nrm((L, S5_GROUPS, S5_CH_PER_GROUP, S5_STATE), S5_STATE ** -0.5),
        's5_d': nrm((L, S5_WIDTH)),
        's5_w_glu': nrm((L, S5_WIDTH, S5_WIDTH), S5_WIDTH ** -0.5),
        's5_b_glu': nrm((L, S5_WIDTH), 0.01),
        'sb_beta_bias': SB_BIAS_INIT + nrm((L, SB_HEADS), 0.1),
        'm2_conv_w': nrm((L, M2_CONV, M2_CONV_DIM), M2_CONV ** -0.5),
        'm2_conv_b': nrm((L, M2_CONV_DIM), 0.01),
        'm2_dt_bias': m2_dt_bias,
        'm2_a_log': m2_a_log,
        'm2_d': 1.0 + nrm((L, M2_HEADS), 0.1),
        'm2_norm': gain((L, M2_INNER)),
        'w_br_s5': nrm((L, S5_WIDTH, D_MODEL), S5_WIDTH ** -0.5),
        'w_br_sb': nrm((L, SB_WIDTH, D_MODEL), SB_WIDTH ** -0.5),
        'w_br_m2': nrm((L, M2_INNER, D_MODEL), M2_INNER ** -0.5),
        'w_out': nrm((L, D_MODEL, D_MODEL), D_MODEL ** -0.5),
        'norm_mem': gain((L, D_MODEL)),
        'mem_wq': nrm((L, D_MODEL, D_MODEL), D_MODEL ** -0.5),
        'mem_wk': nrm((L, D_MODEL, D_MODEL), D_MODEL ** -0.5),
        'mem_wv': nrm((L, D_MODEL, D_MODEL), D_MODEL ** -0.5),
        'mem_wo': nrm((L, D_MODEL, D_MODEL), D_MODEL ** -0.5),
        'norm_mlp': gain((L, D_MODEL)),
        'mlp_w1': nrm((L, D_MODEL, D_FF), D_MODEL ** -0.5),
        'mlp_w2': nrm((L, D_FF, D_MODEL), D_FF ** -0.5),
        'norm_final': gain((D_MODEL,)),
    }


def reference(x_prompt, x_sample, mem_prompt, cache_sb_k, cache_sb_v, state_s5_re, state_s5_im,
              state_ssm, state_conv, cache_mem_k, cache_mem_v, page_table,
              norm_mix, w_in, s5_lambda_re, s5_lambda_im, s5_log_dt, s5_b_re, s5_b_im, s5_c_re, s5_c_im,
              s5_d, s5_w_glu, s5_b_glu, sb_beta_bias, m2_conv_w, m2_conv_b, m2_dt_bias, m2_a_log, m2_d, m2_norm,
              w_br_s5, w_br_sb, w_br_m2, w_out, norm_mem, mem_wq, mem_wk, mem_wv, mem_wo,
              norm_mlp, mlp_w1, mlp_w2, norm_final):
    dec_b, dec_s = x_sample.shape[:2]
    bsz = x_prompt.shape[0]
    past = page_table.shape[1] * PAGE_SIZE
    hp, hs = x_prompt, x_sample
    sbk_p, sbv_p, s5r_p, s5i_p, ssm_p, conv_p, mk_p, mv_p = [], [], [], [], [], [], [], []
    sbk_s, sbv_s, s5r_s, s5i_s, ssm_s, conv_s = [], [], [], [], [], []
    q_pos_s = past + jnp.arange(dec_s, dtype=jnp.int32)
    k_pos_s = jnp.arange(past + dec_s, dtype=jnp.int32)

    for l in range(DEPTH):
        a = -jnp.exp(m2_a_log[l].astype(jnp.float32))
        s5p = (s5_lambda_re[l], s5_lambda_im[l], s5_log_dt[l], s5_b_re[l], s5_b_im[l],
               s5_c_re[l], s5_c_im[l], s5_d[l], s5_w_glu[l], s5_b_glu[l])
        prm = (norm_mix[l], w_in[l], s5p, sb_beta_bias[l], m2_conv_w[l], m2_conv_b[l], m2_dt_bias[l], a,
               m2_d[l], m2_norm[l], w_br_s5[l], w_br_sb[l], w_br_m2[l], w_out[l])

        z5 = jnp.zeros((bsz, S5_GROUPS, S5_STATE), jnp.float32)
        zbuf = jnp.zeros((bsz, M2_CONV - 1, M2_CONV_DIM), hp.dtype)
        hp, k_p, v_p, r_p, i_p, h_p, b_p = mixing_sublayer(hp, prm, z5, z5, zbuf, sb_prompt, ssd_chunked)
        mk, mv = mem_kv(mem_prompt, mem_wk[l], mem_wv[l])
        hp = hp + cross_attn(rmsnorm(hp, norm_mem[l]), mk, mv, mem_wq[l], mem_wo[l])
        hp = hp + sqrelu_mlp(rmsnorm(hp, norm_mlp[l]), mlp_w1[l], mlp_w2[l])
        sbk_p.append(k_p); sbv_p.append(v_p); s5r_p.append(r_p); s5i_p.append(i_p)
        ssm_p.append(h_p); conv_p.append(b_p); mk_p.append(mk); mv_p.append(mv)

        def sb_sample(q, k, v, bias, l=l):
            past_k = cache_sb_k[l][page_table].reshape(dec_b, past, SB_HEADS, SB_HEAD_DIM).astype(k.dtype)
            past_v = cache_sb_v[l][page_table].reshape(dec_b, past, SB_HEADS, SB_HEAD_DIM).astype(v.dtype)
            kk = jnp.concatenate([past_k, k], axis=1)
            vv = jnp.concatenate([past_v, v], axis=1)
            return sb_attend(q, kk, vv, q_pos_s, k_pos_s, bias)

        def ssd_sample(xs, dt, a_, bm, cm, l=l):
            return ssd_recurrent(xs, dt, a_, bm, cm, state_ssm[l])

        hs, k_s, v_s, r_s, i_s, h_s, b_s = mixing_sublayer(
            hs, prm, state_s5_re[l], state_s5_im[l], state_conv[l], sb_sample, ssd_sample)
        hs = hs + cross_attn(rmsnorm(hs, norm_mem[l]), cache_mem_k[l], cache_mem_v[l], mem_wq[l], mem_wo[l])
        hs = hs + sqrelu_mlp(rmsnorm(hs, norm_mlp[l]), mlp_w1[l], mlp_w2[l])
        sbk_s.append(k_s); sbv_s.append(v_s); s5r_s.append(r_s); s5i_s.append(i_s)
        ssm_s.append(h_s); conv_s.append(b_s)

    y_prompt = rmsnorm(hp, norm_final)
    y_sample = rmsnorm(hs, norm_final)
    new_sb_k_prompt = jnp.stack(sbk_p)
    new_sb_v_prompt = jnp.stack(sbv_p)
    new_s5_re_prompt = jnp.stack(s5r_p)
    new_s5_im_prompt = jnp.stack(s5i_p)
    new_ssm_prompt = jnp.stack(ssm_p)
    new_conv_prompt = jnp.stack(conv_p)
    new_mem_k_prompt = jnp.stack(mk_p)
    new_mem_v_prompt = jnp.stack(mv_p)
    new_sb_k_sample = jnp.stack(sbk_s)
    new_sb_v_sample = jnp.stack(sbv_s)
    new_s5_re_sample = jnp.stack(s5r_s)
    new_s5_im_sample = jnp.stack(s5i_s)
    new_ssm_sample = jnp.stack(ssm_s)
    new_conv_sample = jnp.stack(conv_s)
    return (y_prompt, y_sample,
            new_sb_k_prompt, new_sb_v_prompt, new_s5_re_prompt, new_s5_im_prompt,
            new_ssm_prompt, new_conv_prompt, new_mem_k_prompt, new_mem_v_prompt,
            new_sb_k_sample, new_sb_v_sample, new_s5_re_sample, new_s5_im_sample,
            new_ssm_sample, new_conv_sample)
```

```python
import functools
import math

import jax
import jax.numpy as jnp
from jax import lax
from jax.experimental import pallas as pl
from jax.experimental.pallas import tpu as pltpu

F32 = jnp.float32
BF16 = jnp.bfloat16

D_MODEL = 1024
PAGE_SIZE = 128
S5_CH = 16
S5_WIDTH = 512
S5_GROUPS = 32
S5_STATE = 64
S5_LBLK = 2048
S5_T = 16
SB_WIDTH = 512
SB_HEADS = 8
SB_HEAD_DIM = 64
SB_TILE = 256
M2_INNER = 512
M2_HEADS = 8
M2_HEAD_DIM = 64
M2_GROUPS = 2
M2_STATE = 128
M2_CONV = 4
M2_CONV_DIM = 1024
M2_Q = 256
MEM_HEADS = 4
MEM_HEAD_DIM = 256
D_FF = 4096
EPS = 1e-6
IN_MAIN = 3584
IN_DT = 3584
IN_GATE = 3592
VMEM_LIMIT_V7X = 56 * 1024 * 1024


def _cp(*sem):
    return pltpu.CompilerParams(dimension_semantics=sem, vmem_limit_bytes=VMEM_LIMIT_V7X)


def _tile(m, pref):
    t = min(m, pref)
    while m % t:
        t -= 8
    return t


def _full(shape):
    nd = len(shape)
    return pl.BlockSpec(shape, lambda *_: (0,) * nd)


def _layer(shape, l):
    nd = len(shape)
    return pl.BlockSpec((None,) + tuple(shape), lambda *_: (l,) + (0,) * nd)


def _rms(x, g):
    return x * lax.rsqrt(jnp.mean(x * x, axis=-1, keepdims=True) + EPS) * g


def _bdot(a, b):
    return jnp.dot(a.astype(BF16), b.astype(BF16), preferred_element_type=F32)


def _dot_nt(a, b):
    return lax.dot_general(a, b, (((1,), (1,)), ((), ())), preferred_element_type=F32)


def _dot_tn(a, b):
    return lax.dot_general(a, b, (((0,), (0,)), ((), ())), preferred_element_type=F32)


def _softplus(z):
    return jnp.maximum(z, 0.0) + jnp.log(1.0 + jnp.exp(-jnp.abs(z)))


def _sigmoid(z):
    return 1.0 / (1.0 + jnp.exp(-z))


def _silu(z):
    return z * _sigmoid(z)


def _gelu_tanh(x):
    return 0.5 * x * (1.0 + jnp.tanh(math.sqrt(2.0 / math.pi) * (x + 0.044715 * (x * x * x))))


def _in_proj_kernel(x_ref, g_ref, w_ref, u_ref, k_ref, v_ref, qb_ref, kb_ref, vb_ref,
                    zg_ref, xbc_ref, gate_ref, dt_ref):
    xn = _rms(x_ref[...], g_ref[...]).astype(BF16)

    def mm(lo, hi):
        return jnp.dot(xn, w_ref[:, lo:hi], preferred_element_type=F32)

    u_ref[...] = mm(0, 512)
    qb_ref[...] = (mm(512, 1024) * (SB_HEAD_DIM ** -0.5)).astype(BF16)
    k = mm(1024, 1536)
    k_ref[...] = k
    kb_ref[...] = k.astype(BF16)
    v = mm(1536, 2048)
    v_ref[...] = v
    vb_ref[...] = v.astype(BF16)
    zg_ref[...] = mm(2048, 2560)
    xbc_ref[...] = mm(2560, 3584)
    gate_ref[...] = mm(3584, 6656)
    dt_ref[...] = mm(6656, 7168)


def _in_proj(x, g, w_all, l):
    m = x.shape[0]
    tm = _tile(m, 256)
    row = lambda n: pl.BlockSpec((tm, n), lambda i: (i, 0))
    outs = [(512, F32), (512, F32), (512, F32), (512, BF16), (512, BF16), (512, BF16),
            (512, F32), (1024, F32), (3072, F32), (512, F32)]
    return pl.pallas_call(
        _in_proj_kernel,
        grid=(m // tm,),
        in_specs=[row(D_MODEL), _layer((1, D_MODEL), l), _layer((D_MODEL, 7168), l)],
        out_specs=[row(n) for n, _ in outs],
        out_shape=[jax.ShapeDtypeStruct((m, n), dt) for n, dt in outs],
        compiler_params=_cp("parallel"),
        name="in_proj",
    )(x, g, w_all)


def _s5_prep_kernel(lr_row, li_row, lr_col, li_col, ldt_ref, btr_ref, bti_ref, ctr_ref, cti_ref,
                    wcat_ref, poutr_ref, pouti_ref, pw_ref, bbr_ref, bbi_ref):
    dt = jnp.exp(ldt_ref[0])
    lr, li = lr_row[0], li_row[0]
    kk = lax.broadcasted_iota(jnp.int32, (24, 1), 0).astype(F32)
    mag = jnp.exp(kk * (lr * dt))
    ang = kk * (li * dt)
    p_re, p_im = mag * jnp.cos(ang), mag * jnp.sin(ang)
    pw_ref[0, 0] = p_re
    pw_ref[0, 1] = p_im

    def rep(p, lo):
        return jnp.broadcast_to(p[lo:lo + S5_T][:, None, :], (S5_T, S5_CH, S5_STATE)).reshape(256, S5_STATE)

    c_re, c_im = ctr_ref[0], cti_ref[0]
    pr0, pi0 = rep(p_re, 0), rep(p_im, 0)
    l_re = c_re * pr0 - c_im * pi0
    l_im = c_re * pi0 + c_im * pr0
    pr1, pi1 = rep(p_re, 1), rep(p_im, 1)
    poutr_ref[0] = (c_re * pr1 - c_im * pi1).astype(BF16)
    pouti_ref[0] = (-(c_re * pi1 + c_im * pr1)).astype(BF16)

    lrc, lic = lr_col[0], li_col[0]
    m1 = jnp.exp(lrc * dt)
    a_re, a_im = m1 * jnp.cos(lic * dt), m1 * jnp.sin(lic * dt)
    den = lrc * lrc + lic * lic
    f_re = ((a_re - 1.0) * lrc + a_im * lic) / den
    f_im = (a_im * lrc - (a_re - 1.0) * lic) / den
    b_re, b_im = btr_ref[0], bti_ref[0]
    bb_re = f_re * b_re - f_im * b_im
    bb_im = f_re * b_im + f_im * b_re
    bbr_ref[0] = bb_re
    bbi_ref[0] = bb_im

    hp = lax.Precision.HIGHEST
    kt = (jnp.dot(l_re, bb_re, precision=hp, preferred_element_type=F32)
          - jnp.dot(l_im, bb_im, precision=hp, preferred_element_type=F32))
    lane_s = lax.broadcasted_iota(jnp.int32, (1, 256), 1) // S5_CH
    toep = jnp.where(lane_s == 0, kt, 0.0)
    for s in range(1, S5_T):
        sh = jnp.concatenate([jnp.zeros((S5_CH * s, 256), F32), kt[:256 - S5_CH * s]], axis=0)
        toep = jnp.where(lane_s == s, sh, toep)

    ks = (S5_T - 1 - lane_s).astype(F32)
    magc = jnp.exp(ks * (lrc * dt))
    angc = ks * (lic * dt)
    q_re, q_im = magc * jnp.cos(angc), magc * jnp.sin(angc)
    wcat_ref[0, 0:256] = toep.astype(BF16)
    wcat_ref[0, 256:320] = (q_re * bb_re - q_im * bb_im).astype(BF16)
    wcat_ref[0, 320:384] = (q_re * bb_im + q_im * bb_re).astype(BF16)


def _s5_prep(lam_re, lam_im, log_dt, b_re, b_im, c_re, c_im):
    g = S5_GROUPS
    per = lambda *shape: pl.BlockSpec((1,) + shape, lambda i: (i,) + (0,) * len(shape))
    tile_b = lambda b: jnp.tile(b, (1, 1, S5_T))
    tile_c = lambda c: jnp.tile(c, (1, S5_T, 1))
    return pl.pallas_call(
        _s5_prep_kernel,
        grid=(g,),
        in_specs=[per(1, 64), per(1, 64), per(64, 1), per(64, 1), per(1, 1),
                  per(64, 256), per(64, 256), per(256, 64), per(256, 64)],
        out_specs=[per(384, 256), per(256, 64), per(256, 64), per(2, 24, 64), per(64, 256), per(64, 256)],
        out_shape=[jax.ShapeDtypeStruct((g, 384, 256), BF16),
                   jax.ShapeDtypeStruct((g, 256, 64), BF16),
                   jax.ShapeDtypeStruct((g, 256, 64), BF16),
                   jax.ShapeDtypeStruct((g, 2, 24, 64), F32),
                   jax.ShapeDtypeStruct((g, 64, 256), F32),
                   jax.ShapeDtypeStruct((g, 64, 256), F32)],
        compiler_params=_cp("parallel"),
        name="s5_prep",
    )(lam_re[:, None, :], lam_im[:, None, :], lam_re[:, :, None], lam_im[:, :, None],
      log_dt[:, None, None], tile_b(b_re), tile_b(b_im), tile_c(c_re), tile_c(c_im))


def _rows(u_ref, s, j, n):
    return u_ref[0, pl.ds(s * 4 + j, n, stride=4 * S5_T), :]


def _s5_prompt_kernel(u_ref, wcat_ref, poutr_ref, pouti_ref, ar_ref, ai_ref, d_ref, wglu_ref, bglu_ref,
                      y_ref, sr_ref, si_ref, v_s, yt_s, hr_s, hi_s, cr_s, ci_s):
    n = v_s.shape[2]
    lb = pl.program_id(1)

    @pl.when(lb == 0)
    def _():
        cr_s[...] = jnp.zeros_like(cr_s)
        ci_s[...] = jnp.zeros_like(ci_s)

    for s in range(S5_T):
        for j in range(4):
            v_s[s, j * 128:(j + 1) * 128, :] = _rows(u_ref, s, j, n).T.astype(BF16)

    for gp in range(S5_GROUPS // 2):
        res = []
        for g in (2 * gp, 2 * gp + 1):
            vg = v_s[:, g * S5_CH:(g + 1) * S5_CH, :].reshape(S5_T * S5_CH, n)
            r = jnp.dot(wcat_ref[g], vg, preferred_element_type=F32)
            yt_s[g] = r[0:256]
            res.append(r)
        hr_s[:, gp * 128:(gp + 1) * 128] = jnp.concatenate([res[0][256:320], res[1][256:320]], axis=0).T
        hi_s[:, gp * 128:(gp + 1) * 128] = jnp.concatenate([res[0][320:384], res[1][320:384]], axis=0).T

    a_re, a_im = ar_ref[...], ai_ref[...]

    def step(i, carry):
        h_re, h_im = carry
        s_re = hr_s[pl.ds(i, 1), :]
        s_im = hi_s[pl.ds(i, 1), :]
        hr_s[pl.ds(i, 1), :] = h_re
        hi_s[pl.ds(i, 1), :] = h_im
        return (a_re * h_re - a_im * h_im + s_re, a_re * h_im + a_im * h_re + s_im)

    h_re, h_im = lax.fori_loop(0, n, step, (cr_s[...], ci_s[...]))
    cr_s[...] = h_re
    ci_s[...] = h_im
    sr_ref[0] = h_re
    si_ref[0] = h_im

    for gp in range(S5_GROUPS // 2):
        ht_re = hr_s[:, gp * 128:(gp + 1) * 128].T.astype(BF16)
        ht_im = hi_s[:, gp * 128:(gp + 1) * 128].T.astype(BF16)
        for j, g in enumerate((2 * gp, 2 * gp + 1)):
            yt_s[g] += (jnp.dot(poutr_ref[g], ht_re[64 * j:64 * j + 64], preferred_element_type=F32)
                        + jnp.dot(pouti_ref[g], ht_im[64 * j:64 * j + 64], preferred_element_type=F32))

    for t in range(S5_T):
        yt = yt_s[:, t * S5_CH:(t + 1) * S5_CH, :].reshape(S5_WIDTH, n).T
        y = yt + d_ref[...] * jnp.concatenate([_rows(u_ref, t, j, n) for j in range(4)], axis=1)
        gl = _gelu_tanh(y)
        o = gl * _sigmoid(jnp.dot(gl.astype(BF16), wglu_ref[...], preferred_element_type=F32) + bglu_ref[...])
        for j in range(4):
            y_ref[0, pl.ds(t * 4 + j, n, stride=4 * S5_T), :] = o[:, j * 128:(j + 1) * 128]


def _s5_prompt(u, wcat, poutr, pouti, a16r, a16i, d, wglu, bglu, l):
    b, sl, _ = u.shape
    lblk = _tile(sl, S5_LBLK)
    n = lblk // S5_T
    return pl.pallas_call(
        _s5_prompt_kernel,
        grid=(b, sl // lblk),
        in_specs=[pl.BlockSpec((1, lblk * 4, 128), lambda i, j: (i, j, 0)),
                  _full((S5_GROUPS, 384, 256)), _full((S5_GROUPS, 256, 64)), _full((S5_GROUPS, 256, 64)),
                  _full((1, 2048)), _full((1, 2048)),
                  _layer((1, S5_WIDTH), l), _layer((S5_WIDTH, S5_WIDTH), l), _layer((1, S5_WIDTH), l)],
        out_specs=[pl.BlockSpec((1, lblk * 4, 128), lambda i, j: (i, j, 0)),
                   pl.BlockSpec((1, 1, 2048), lambda i, j: (i, 0, 0)),
                   pl.BlockSpec((1, 1, 2048), lambda i, j: (i, 0, 0))],
        out_shape=[jax.ShapeDtypeStruct((b, sl * 4, 128), F32),
                   jax.ShapeDtypeStruct((b, 1, 2048), F32),
                   jax.ShapeDtypeStruct((b, 1, 2048), F32)],
        scratch_shapes=[pltpu.VMEM((S5_T, S5_WIDTH, n), BF16),
                        pltpu.VMEM((S5_GROUPS, 256, n), F32),
                        pltpu.VMEM((n, 2048), F32), pltpu.VMEM((n, 2048), F32),
                        pltpu.VMEM((1, 2048), F32), pltpu.VMEM((1, 2048), F32)],
        compiler_params=_cp("parallel", "arbitrary"),
        name="s5_prompt",
    )(u.reshape(b, sl * 4, 128), wcat, poutr, pouti, a16r, a16i, d, wglu, bglu)


def _s5_step_kernel(u_ref, h0r_ref, h0i_ref, ar_ref, ai_ref, bbr_ref, bbi_ref, ccr_ref, cci_ref,
                    d_ref, wglu_ref, bglu_ref, y_ref, hr_ref, hi_ref):
    u = u_ref[...]
    ub = u.astype(BF16)
    a_re, a_im = ar_ref[...], ai_ref[...]
    h0r, h0i = h0r_ref[...], h0i_ref[...]
    h_re = a_re * h0r - a_im * h0i + jnp.dot(ub, bbr_ref[...], preferred_element_type=F32)
    h_im = a_re * h0i + a_im * h0r + jnp.dot(ub, bbi_ref[...], preferred_element_type=F32)
    hr_ref[...] = h_re
    hi_ref[...] = h_im
    y = (jnp.dot(h_re.astype(BF16), ccr_ref[...], preferred_element_type=F32)
         - jnp.dot(h_im.astype(BF16), cci_ref[...], preferred_element_type=F32) + d_ref[...] * u)
    gl = _gelu_tanh(y)
    y_ref[...] = gl * _sigmoid(jnp.dot(gl.astype(BF16), wglu_ref[...], preferred_element_type=F32) + bglu_ref[...])


def _s5_step(u, h0r, h0i, a1r, a1i, bbr, bbi, ccr, cci, d, wglu, bglu, l):
    m = u.shape[0]
    return pl.pallas_call(
        _s5_step_kernel,
        grid=(1,),
        in_specs=[_full((m, S5_WIDTH)), _layer((m, 2048), l), _layer((m, 2048), l),
                  _full((1, 2048)), _full((1, 2048)),
                  _full((S5_WIDTH, 2048)), _full((S5_WIDTH, 2048)), _full((2048, S5_WIDTH)), _full((2048, S5_WIDTH)),
                  _layer((1, S5_WIDTH), l), _layer((S5_WIDTH, S5_WIDTH), l), _layer((1, S5_WIDTH), l)],
        out_specs=[_full((m, S5_WIDTH)), _full((m, 2048)), _full((m, 2048))],
        out_shape=[jax.ShapeDtypeStruct((m, S5_WIDTH), F32),
                   jax.ShapeDtypeStruct((m, 2048), F32), jax.ShapeDtypeStruct((m, 2048), F32)],
        compiler_params=_cp("arbitrary"),
        name="s5_step",
    )(u, h0r, h0i, a1r, a1i, bbr, bbi, ccr, cci, d, wglu, bglu)


def _sb_prompt_kernel(bias_ref, q_ref, k_ref, v_ref, tri_ref, o_ref):
    t = SB_TILE
    pair = pl.program_id(1)
    qi = pl.program_id(2)
    q = q_ref[0]
    lane = lax.broadcasted_iota(jnp.int32, (t, 128), 1)
    row = lax.broadcasted_iota(jnp.int32, (t, t), 0)
    col = lax.broadcasted_iota(jnp.int32, (t, t), 1)
    valid = col < row
    outs = []
    for hh in range(2):
        qh = jnp.where((lane < SB_HEAD_DIM) == (hh == 0), q, jnp.zeros_like(q))
        bias = bias_ref[pair * 2 + hh]

        def block(kb, carry, diag):
            acc, c = carry
            start = pl.multiple_of(kb * t, t)
            z = _dot_nt(qh, k_ref[0, pl.ds(start, t), :]) + bias
            lk = -_softplus(z)
            if diag:
                lk = jnp.where(valid, lk, 0.0)
            hi = lk.astype(BF16)
            lo = (lk - hi.astype(F32)).astype(BF16)
            r = jnp.dot(jnp.concatenate([hi, lo], axis=1), tri_ref[...], preferred_element_type=F32) + c
            w = jnp.exp(z + r)
            if diag:
                w = jnp.where(valid, w, 0.0)
            acc = acc + jnp.dot(w.astype(BF16), v_ref[0, pl.ds(start, t), :], preferred_element_type=F32)
            return acc, r[:, 0:1]

        carry = block(qi, (jnp.zeros((t, 128), F32), jnp.zeros((t, 1), F32)), True)
        carry = lax.fori_loop(0, qi, lambda j, cr: block(qi - 1 - j, cr, False), carry)
        outs.append(carry[0])
    o_ref[0] = jnp.where(lane < SB_HEAD_DIM, outs[0], outs[1])


def _sb_prompt(qb, kb, vb, bias, tri):
    b, sl, _ = qb.shape
    t = SB_TILE
    return pl.pallas_call(
        _sb_prompt_kernel,
        grid=(b, SB_HEADS // 2, sl // t),
        in_specs=[pl.BlockSpec(memory_space=pltpu.SMEM),
                  pl.BlockSpec((1, t, 128), lambda i, p, j: (i, j, p)),
                  pl.BlockSpec((1, sl, 128), lambda i, p, j: (i, 0, p)),
                  pl.BlockSpec((1, sl, 128), lambda i, p, j: (i, 0, p)),
                  _full((2 * t, t))],
        out_specs=pl.BlockSpec((1, t, 128), lambda i, p, j: (i, j, p)),
        out_shape=jax.ShapeDtypeStruct((b, sl, SB_WIDTH), F32),
        compiler_params=_cp("parallel", "parallel", "arbitrary"),
        name="sb_prompt",
    )(bias, qb, kb, vb, tri)


def _suffix_sum_rows(x):
    n = x.shape[0]
    row = lax.broadcasted_iota(jnp.int32, (n, 1), 0)
    k = 1
    while k < n:
        x = x + jnp.where(row < n - k, pltpu.roll(x, n - k, 0), 0.0)
        k *= 2
    return x


def _sb_decode_kernel(npg, pt_ref, q_ref, bias_ref, *refs):
    k_refs, v_refs = refs[:npg], refs[npg:2 * npg]
    o_ref, acc_s, c_s = refs[2 * npg:]
    j = pl.program_id(1)

    @pl.when(j == 0)
    def _():
        acc_s[...] = jnp.zeros_like(acc_s)
        c_s[...] = jnp.zeros_like(c_s)

    q = q_ref[0]
    bias = bias_ref[...]
    head_of_row = lax.broadcasted_iota(jnp.int32, (SB_WIDTH, 128), 0) // SB_HEAD_DIM
    e_mat = (head_of_row == lax.broadcasted_iota(jnp.int32, (SB_WIDTH, 128), 1)).astype(BF16)
    head_of_col = lax.broadcasted_iota(jnp.int32, (128, SB_WIDTH), 1) // SB_HEAD_DIM
    e_t = (head_of_col == lax.broadcasted_iota(jnp.int32, (128, SB_WIDTH), 0)).astype(BF16)
    acc = acc_s[...]
    c = c_s[...]
    for i in range(npg - 1, -1, -1):
        prod = (k_refs[i][...] * q).astype(BF16)
        z = jnp.dot(prod, e_mat, preferred_element_type=F32) + bias
        r = _suffix_sum_rows(-_softplus(z)) + c
        w = jnp.exp(z + r)
        c = r[0:1, :]
        wv = jnp.dot(w.astype(BF16), e_t, preferred_element_type=F32) * v_refs[i][...]
        acc = acc + wv.reshape(PAGE_SIZE // 8, 8, SB_WIDTH).sum(axis=0)
    acc_s[...] = acc
    c_s[...] = c
    o_ref[0] = acc.sum(axis=0, keepdims=True)


def _sb_decode(q, bias_row, cache_k, cache_v, page_table, l):
    m = q.shape[0]
    n_pages = page_table.shape[1]
    npg = 16
    while n_pages % npg:
        npg //= 2
    steps = n_pages // npg

    def page(i):
        return pl.BlockSpec((None, None, PAGE_SIZE, SB_WIDTH),
                            lambda b, j, pt: (l, pt[b, (steps - 1 - j) * npg + i], 0, 0))

    grid_spec = pltpu.PrefetchScalarGridSpec(
        num_scalar_prefetch=1,
        grid=(m, steps),
        in_specs=[pl.BlockSpec((1, 1, SB_WIDTH), lambda b, j, pt: (b, 0, 0)),
                  pl.BlockSpec((1, 128), lambda b, j, pt: (0, 0))]
                 + [page(i) for i in range(npg)] * 2,
        out_specs=pl.BlockSpec((1, 1, SB_WIDTH), lambda b, j, pt: (b, 0, 0)),
        scratch_shapes=[pltpu.VMEM((8, SB_WIDTH), F32), pltpu.VMEM((1, 128), F32)],
    )
    depth, n_pool = cache_k.shape[:2]
    ck = cache_k.reshape(depth, n_pool, PAGE_SIZE, SB_WIDTH)
    cv = cache_v.reshape(depth, n_pool, PAGE_SIZE, SB_WIDTH)
    out = pl.pallas_call(
        functools.partial(_sb_decode_kernel, npg),
        grid_spec=grid_spec,
        out_shape=jax.ShapeDtypeStruct((m, 1, SB_WIDTH), F32),
        compiler_params=_cp("parallel", "arbitrary"),
        name="sb_decode",
    )(page_table, q.reshape(m, 1, SB_WIDTH), bias_row, *([ck] * npg), *([cv] * npg))
    return out.reshape(m, SB_WIDTH)


def _cumsum_rows(x):
    n = x.shape[0]
    row = lax.broadcasted_iota(jnp.int32, (n, 1), 0)
    k = 1
    while k < n:
        x = x + jnp.where(row >= k, pltpu.roll(x, k, 0), 0.0)
        k *= 2
    return x


def _ssd_prompt_kernel(xbc_ref, dt_ref, zg_ref, cw_ref, cb_ref, dtb_ref, a_ref, d_ref, nrm_ref,
                       y_ref, st_ref, cv_ref, prev_s, st_s):
    q = xbc_ref.shape[1]
    c = pl.program_id(1)

    @pl.when(c == 0)
    def _():
        prev_s[...] = jnp.zeros_like(prev_s)
        st_s[...] = jnp.zeros_like(st_s)

    u = xbc_ref[0]
    prev = prev_s[...]
    row8 = lax.broadcasted_iota(jnp.int32, (8, 1), 0)
    conv = cb_ref[...] + cw_ref[M2_CONV - 1:M2_CONV, :] * u
    for k in range(1, M2_CONV):
        ru = pltpu.roll(u, k, 0)
        top = jnp.where(row8 < k, pltpu.roll(prev, k, 0), ru[0:8])
        conv = conv + cw_ref[M2_CONV - 1 - k:M2_CONV - k, :] * jnp.concatenate([top, ru[8:]], axis=0)
    prev_s[...] = u[q - 8:q]
    xc = _silu(conv)
    xs = xc[:, 0:M2_INNER]
    dt = _softplus(dt_ref[0] + dtb_ref[...])
    acum = _cumsum_rows(dt * a_ref[...])
    a_last = acum[q - 1:q, :]
    xdt = xs * dt
    xw = (xdt * jnp.exp(a_last - acum)).astype(BF16)
    xdt_b = xdt.astype(BF16)
    e_acum = jnp.exp(acum)
    rows = lax.broadcasted_iota(jnp.int32, (q, q), 0)
    cols = lax.broadcasted_iota(jnp.int32, (q, q), 1)
    causal = cols <= rows
    lane = lax.broadcasted_iota(jnp.int32, (q, 128), 1)
    y_parts = []
    for g in range(M2_GROUPS):
        bm = xc[:, M2_INNER + g * M2_STATE:M2_INNER + (g + 1) * M2_STATE].astype(BF16)
        cm = xc[:, M2_INNER + (M2_GROUPS + g) * M2_STATE:M2_INNER + (M2_GROUPS + g + 1) * M2_STATE].astype(BF16)
        cb = _dot_nt(cm, bm)
        st = st_s[g]
        y_off = jnp.dot(cm, st.astype(BF16), preferred_element_type=F32) * e_acum[:, g * 256:(g + 1) * 256]
        st_s[g] = st * jnp.exp(a_last[:, g * 256:(g + 1) * 256]) + _dot_tn(bm, xw[:, g * 256:(g + 1) * 256])
        for pp in range(2):
            p = 2 * g + pp
            a_pair = acum[:, p * 128:(p + 1) * 128]
            a_t = a_pair.T
            yd = []
            for hh in range(2):
                seg = a_pair[:, 64 * hh:64 * hh + 1] - a_t[64 * hh:64 * hh + 1, :]
                mix = jnp.where(causal, cb * jnp.exp(jnp.minimum(seg, 0.0)), 0.0).astype(BF16)
                yd.append(jnp.dot(mix, xdt_b[:, p * 128:(p + 1) * 128], preferred_element_type=F32))
            y_parts.append(jnp.where(lane < M2_HEAD_DIM, yd[0], yd[1]) + y_off[:, pp * 128:(pp + 1) * 128])
    y = jnp.concatenate(y_parts, axis=1) + d_ref[...] * xs
    y_ref[0] = _rms(y * _silu(zg_ref[0]), nrm_ref[...])

    @pl.when(c == pl.num_programs(1) - 1)
    def _():
        for g in range(M2_GROUPS):
            st_ref[0, g * 256:(g + 1) * 256, :] = st_s[g].T
        cv_ref[0] = xbc_ref[0, q - (M2_CONV - 1):q, :]


def _ssd_prompt(xbc, dt_raw, zg, cw, cb, dtb, a, d, nrm, l):
    b, sl, _ = xbc.shape
    q = _tile(sl, M2_Q)
    tok = lambda n: pl.BlockSpec((1, q, n), lambda i, j: (i, j, 0))
    return pl.pallas_call(
        _ssd_prompt_kernel,
        grid=(b, sl // q),
        in_specs=[tok(M2_CONV_DIM), tok(M2_INNER), tok(M2_INNER),
                  _layer((M2_CONV, M2_CONV_DIM), l), _layer((1, M2_CONV_DIM), l),
                  _layer((1, M2_INNER), l), _layer((1, M2_INNER), l), _layer((1, M2_INNER), l),
                  _layer((1, M2_INNER), l)],
        out_specs=[tok(M2_INNER),
                   pl.BlockSpec((1, M2_INNER, M2_STATE), lambda i, j: (i, 0, 0)),
                   pl.BlockSpec((1, M2_CONV - 1, M2_CONV_DIM), lambda i, j: (i, 0, 0))],
        out_shape=[jax.ShapeDtypeStruct((b, sl, M2_INNER), F32),
                   jax.ShapeDtypeStruct((b, M2_INNER, M2_STATE), F32),
                   jax.ShapeDtypeStruct((b, M2_CONV - 1, M2_CONV_DIM), F32)],
        scratch_shapes=[pltpu.VMEM((8, M2_CONV_DIM), F32), pltpu.VMEM((M2_GROUPS, M2_STATE, 256), F32)],
        compiler_params=_cp("parallel", "arbitrary"),
        name="ssd_prompt",
    )(xbc, dt_raw, zg, cw, cb, dtb, a, d, nrm)


def _ssd_step_kernel(xbc_ref, cs_ref, dt_ref, zg_ref, st_ref, cw_ref, cb_ref, dtb_ref, a_ref, d_ref, nrm_ref,
                     y_ref, sto_ref, cvo_ref):
    nb = xbc_ref.shape[0]
    x = xbc_ref[...]
    b0, b1, b2 = cs_ref[:, 0, :], cs_ref[:, 1, :], cs_ref[:, 2, :]
    conv = (cb_ref[...] + cw_ref[0:1, :] * b0 + cw_ref[1:2, :] * b1 + cw_ref[2:3, :] * b2 + cw_ref[3:4, :] * x)
    cvo_ref[:, 0, :] = b1
    cvo_ref[:, 1, :] = b2
    cvo_ref[:, 2, :] = x
    xc = _silu(conv)
    xs = xc[:, 0:M2_INNER]
    dt = _softplus(dt_ref[...] + dtb_ref[...])
    dec = jnp.exp(dt * a_ref[...])
    pad = jnp.zeros((128 - nb, M2_INNER), F32)
    xdt_t = jnp.concatenate([xs * dt, pad], axis=0).T
    dec_t = jnp.concatenate([dec, pad], axis=0).T
    hrow = lax.broadcasted_iota(jnp.int32, (M2_INNER, 1), 0)
    lane = lax.broadcasted_iota(jnp.int32, (1, 128), 1)
    ycols = jnp.zeros((M2_INNER, 128), F32)
    for b in range(nb):
        brow = jnp.where(hrow < 256, xc[b:b + 1, 512:640], xc[b:b + 1, 640:768])
        crow = jnp.where(hrow < 256, xc[b:b + 1, 768:896], xc[b:b + 1, 896:1024])
        h_new = dec_t[:, b:b + 1] * st_ref[b] + xdt_t[:, b:b + 1] * brow
        sto_ref[b] = h_new
        ycols = jnp.where(lane == b, jnp.sum(h_new * crow, axis=1, keepdims=True), ycols)
    y = ycols.T[0:nb] + d_ref[...] * xs
    y_ref[...] = _rms(y * _silu(zg_ref[...]), nrm_ref[...])


def _ssd_step(xbc, conv_state, dt_raw, zg, ssm_state, cw, cb, dtb, a, d, nrm, l):
    m = xbc.shape[0]
    nb = 8
    tok = lambda n: pl.BlockSpec((nb, n), lambda i: (i, 0))
    depth = ssm_state.shape[0]
    st = ssm_state.reshape(depth, m, M2_INNER, M2_STATE)
    return pl.pallas_call(
        _ssd_step_kernel,
        grid=(m // nb,),
        in_specs=[tok(M2_CONV_DIM),
                  pl.BlockSpec((None, nb, M2_CONV - 1, M2_CONV_DIM), lambda i: (l, i, 0, 0)),
                  tok(M2_INNER), tok(M2_INNER),
                  pl.BlockSpec((None, nb, M2_INNER, M2_STATE), lambda i: (l, i, 0, 0)),
                  _layer((M2_CONV, M2_CONV_DIM), l), _layer((1, M2_CONV_DIM), l),
                  _layer((1, M2_INNER), l), _layer((1, M2_INNER), l), _layer((1, M2_INNER), l),
                  _layer((1, M2_INNER), l)],
        out_specs=[tok(M2_INNER),
                   pl.BlockSpec((nb, M2_INNER, M2_STATE), lambda i: (i, 0, 0)),
                   pl.BlockSpec((nb, M2_CONV - 1, M2_CONV_DIM), lambda i: (i, 0, 0))],
        out_shape=[jax.ShapeDtypeStruct((m, M2_INNER), F32),
                   jax.ShapeDtypeStruct((m, M2_INNER, M2_STATE), F32),
                   jax.ShapeDtypeStruct((m, M2_CONV - 1, M2_CONV_DIM), F32)],
        compiler_params=_cp("parallel"),
        name="ssd_step",
    )(xbc, conv_state, dt_raw, zg, st, cw, cb, dtb, a, d, nrm)


def _merge_kernel(h_ref, y5_ref, ysb_ref, ym_ref, gate_ref, w5_ref, wsb_ref, wm_ref, wo_ref, o_ref):
    def branch(y_ref, w_ref, lo):
        return _sigmoid(gate_ref[:, lo:lo + D_MODEL]) * jnp.dot(
            y_ref[...].astype(BF16), w_ref[...], preferred_element_type=F32)

    merged = branch(y5_ref, w5_ref, 0) + branch(ysb_ref, wsb_ref, D_MODEL) + branch(ym_ref, wm_ref, 2 * D_MODEL)
    o_ref[...] = h_ref[...] + jnp.dot(merged.astype(BF16), wo_ref[...], preferred_element_type=F32)


def _merge(h, y5, ysb, ym, gates, w5, wsb, wm, wo, l):
    m = h.shape[0]
    tm = _tile(m, 512)
    row = lambda n: pl.BlockSpec((tm, n), lambda i: (i, 0))
    return pl.pallas_call(
        _merge_kernel,
        grid=(m // tm,),
        in_specs=[row(D_MODEL), row(512), row(512), row(512), row(3 * D_MODEL),
                  _layer((512, D_MODEL), l), _layer((512, D_MODEL), l), _layer((512, D_MODEL), l),
                  _layer((D_MODEL, D_MODEL), l)],
        out_specs=row(D_MODEL),
        out_shape=jax.ShapeDtypeStruct((m, D_MODEL), F32),
        compiler_params=_cp("parallel"),
        name="merge",
    )(h, y5, ysb, ym, gates, w5, wsb, wm, wo)


def _mem_kv_kernel(x_ref, wk_ref, wv_ref, k_ref, v_ref, kb_ref, vb_ref):
    xb = x_ref[...].astype(BF16)
    k = jnp.dot(xb, wk_ref[...], preferred_element_type=F32)
    v = jnp.dot(xb, wv_ref[...], preferred_element_type=F32)
    k_ref[...] = k
    v_ref[...] = v
    kb_ref[...] = k.astype(BF16)
    vb_ref[...] = v.astype(BF16)


def _mem_kv(mem, wk, wv, l):
    m = mem.shape[0]
    tm = _tile(m, 512)
    row = pl.BlockSpec((tm, D_MODEL), lambda i: (i, 0))
    return pl.pallas_call(
        _mem_kv_kernel,
        grid=(m // tm,),
        in_specs=[row, _layer((D_MODEL, D_MODEL), l), _layer((D_MODEL, D_MODEL), l)],
        out_specs=[row] * 4,
        out_shape=[jax.ShapeDtypeStruct((m, D_MODEL), F32)] * 2 + [jax.ShapeDtypeStruct((m, D_MODEL), BF16)] * 2,
        compiler_params=_cp("parallel"),
        name="mem_kv",
    )(mem, wk, wv)


def _xattn_prompt_kernel(h_ref, g_ref, wq_ref, mk_ref, mv_ref, wo_ref, o_ref):
    h = h_ref[0]
    xn = _rms(h, g_ref[...]).astype(BF16)
    q = (jnp.dot(xn, wq_ref[...], preferred_element_type=F32) * (MEM_HEAD_DIM ** -0.5)).astype(BF16)
    heads = []
    for hd in range(MEM_HEADS):
        sl = slice(hd * MEM_HEAD_DIM, (hd + 1) * MEM_HEAD_DIM)
        s = _dot_nt(q[:, sl], mk_ref[0, :, sl])
        e = jnp.exp(s - jnp.max(s, axis=-1, keepdims=True))
        p = e / jnp.sum(e, axis=-1, keepdims=True)
        heads.append(jnp.dot(p.astype(BF16), mv_ref[0, :, sl], preferred_element_type=F32))
    o = jnp.concatenate(heads, axis=1).astype(BF16)
    o_ref[0] = h + jnp.dot(o, wo_ref[...], preferred_element_type=F32)


def _xattn_prompt(h, g, wq, mkb, mvb, wo, l):
    b, sl, _ = h.shape
    tq = _tile(sl, 512)
    mt = mkb.shape[1]
    return pl.pallas_call(
        _xattn_prompt_kernel,
        grid=(b, sl // tq),
        in_specs=[pl.BlockSpec((1, tq, D_MODEL), lambda i, j: (i, j, 0)),
                  _layer((1, D_MODEL), l), _layer((D_MODEL, D_MODEL), l),
                  pl.BlockSpec((1, mt, D_MODEL), lambda i, j: (i, 0, 0)),
                  pl.BlockSpec((1, mt, D_MODEL), lambda i, j: (i, 0, 0)),
                  _layer((D_MODEL, D_MODEL), l)],
        out_specs=pl.BlockSpec((1, tq, D_MODEL), lambda i, j: (i, j, 0)),
        out_shape=jax.ShapeDtypeStruct((b, sl, D_MODEL), F32),
        compiler_params=_cp("parallel", "parallel"),
        name="xattn_prompt",
    )(h, g, wq, mkb, mvb, wo)


def _xattn_sample_kernel(h_ref, g_ref, wq_ref, mk_ref, mv_ref, wo_ref, o_ref, q_s, att_s):
    b = pl.program_id(0)

    @pl.when(b == 0)
    def _():
        xn = _rms(h_ref[...], g_ref[...]).astype(BF16)
        q_s[...] = jnp.dot(xn, wq_ref[...], preferred_element_type=F32) * (MEM_HEAD_DIM ** -0.5)

    head_of_row = lax.broadcasted_iota(jnp.int32, (D_MODEL, 128), 0) // MEM_HEAD_DIM
    e_mat = (head_of_row == lax.broadcasted_iota(jnp.int32, (D_MODEL, 128), 1)).astype(BF16)
    head_of_col = lax.broadcasted_iota(jnp.int32, (128, D_MODEL), 1) // MEM_HEAD_DIM
    e_t = (head_of_col == lax.broadcasted_iota(jnp.int32, (128, D_MODEL), 0)).astype(BF16)
    q = q_s[pl.ds(b, 1), :]
    s = jnp.dot((mk_ref[...] * q).astype(BF16), e_mat, preferred_element_type=F32)
    e = jnp.exp(s - jnp.max(s, axis=0, keepdims=True))
    p = e / jnp.sum(e, axis=0, keepdims=True)
    pv = jnp.dot(p.astype(BF16), e_t, preferred_element_type=F32) * mv_ref[...]
    att_s[pl.ds(b, 1), :] = jnp.sum(pv, axis=0, keepdims=True)

    @pl.when(b == pl.num_programs(0) - 1)
    def _():
        o_ref[...] = h_ref[...] + jnp.dot(att_s[...].astype(BF16), wo_ref[...], preferred_element_type=F32)


def _xattn_sample(h, g, wq, cache_k, cache_v, wo, l):
    m = h.shape[0]
    depth, _, mt = cache_k.shape[:3]
    ck = cache_k.reshape(depth, m, mt, D_MODEL)
    cv = cache_v.reshape(depth, m, mt, D_MODEL)
    kv = pl.BlockSpec((None, None, mt, D_MODEL), lambda i: (l, i, 0, 0))
    return pl.pallas_call(
        _xattn_sample_kernel,
        grid=(m,),
        in_specs=[_full((m, D_MODEL)), _layer((1, D_MODEL), l), _layer((D_MODEL, D_MODEL), l), kv, kv,
                  _layer((D_MODEL, D_MODEL), l)],
        out_specs=_full((m, D_MODEL)),
        out_shape=jax.ShapeDtypeStruct((m, D_MODEL), F32),
        scratch_shapes=[pltpu.VMEM((m, D_MODEL), F32), pltpu.VMEM((m, D_MODEL), F32)],
        compiler_params=_cp("arbitrary"),
        name="xattn_sample",
    )(h, g, wq, ck, cv, wo)


def _mlp_kernel(h_ref, g_ref, w1_ref, w2_ref, o_ref):
    h = h_ref[...]
    xn = _rms(h, g_ref[...]).astype(BF16)
    acc = h
    for j in range(D_FF // 1024):
        a = jnp.maximum(jnp.dot(xn, w1_ref[:, j * 1024:(j + 1) * 1024], preferred_element_type=F32), 0.0)
        acc = acc + jnp.dot((a * a).astype(BF16), w2_ref[j * 1024:(j + 1) * 1024, :], preferred_element_type=F32)
    o_ref[...] = acc


def _mlp(h, g, w1, w2, l):
    m = h.shape[0]
    tm = _tile(m, 512)
    row = pl.BlockSpec((tm, D_MODEL), lambda i: (i, 0))
    return pl.pallas_call(
        _mlp_kernel,
        grid=(m // tm,),
        in_specs=[row, _layer((1, D_MODEL), l), _layer((D_MODEL, D_FF), l), _layer((D_FF, D_MODEL), l)],
        out_specs=row,
        out_shape=jax.ShapeDtypeStruct((m, D_MODEL), F32),
        compiler_params=_cp("parallel"),
        name="mlp",
    )(h, g, w1, w2)


def _norm_kernel(h_ref, g_ref, o_ref):
    o_ref[...] = _rms(h_ref[...], g_ref[...])


def _final_norm(h, g):
    m = h.shape[0]
    tm = _tile(m, 1024)
    row = pl.BlockSpec((tm, D_MODEL), lambda i: (i, 0))
    return pl.pallas_call(
        _norm_kernel,
        grid=(m // tm,),
        in_specs=[row, _full((1, D_MODEL))],
        out_specs=row,
        out_shape=jax.ShapeDtypeStruct((m, D_MODEL), F32),
        compiler_params=_cp("parallel"),
        name="final_norm",
    )(h, g)


def _block_diag(blocks):
    g, r, c = blocks.shape
    eye = jnp.eye(g, dtype=blocks.dtype)
    return (blocks[:, :, None, :] * eye[:, None, :, None]).reshape(g * r, g * c)


def kernel(x_prompt, x_sample, mem_prompt, cache_sb_k, cache_sb_v, state_s5_re, state_s5_im, state_ssm, state_conv, cache_mem_k, cache_mem_v, page_table, norm_mix, w_in, s5_lambda_re, s5_lambda_im, s5_log_dt, s5_b_re, s5_b_im, s5_c_re, s5_c_im, s5_d, s5_w_glu, s5_b_glu, sb_beta_bias, m2_conv_w, m2_conv_b, m2_dt_bias, m2_a_log, m2_d, m2_norm, w_br_s5, w_br_sb, w_br_m2, w_out, norm_mem, mem_wq, mem_wk, mem_wv, mem_wo, norm_mlp, mlp_w1, mlp_w2, norm_final):
    depth = w_in.shape[0]
    bsz, sl, _ = x_prompt.shape
    dec_b = x_sample.shape[0]
    mt = mem_prompt.shape[1]
    rep = lambda p: jnp.repeat(p, M2_HEAD_DIM, axis=-1)[:, None, :]
    row = lambda p: p[:, None, :]

    w_all = jnp.concatenate([w_in[:, :, :IN_MAIN], w_in[:, :, IN_GATE:],
                             jnp.repeat(w_in[:, :, IN_DT:IN_GATE], M2_HEAD_DIM, axis=-1)], axis=-1).astype(BF16)
    wglu, w5, wsb, wm, wo = (w.astype(BF16) for w in (s5_w_glu, w_br_s5, w_br_sb, w_br_m2, w_out))
    wq, wk, wv, wmo, w1, w2 = (w.astype(BF16) for w in (mem_wq, mem_wk, mem_wv, mem_wo, mlp_w1, mlp_w2))
    g_mix, g_mem, g_mlp = row(norm_mix), row(norm_mem), row(norm_mlp)
    s5d, bglu, cb = row(s5_d), row(s5_b_glu), row(m2_conv_b)
    dtb, a_neg, m2d, m2n = rep(m2_dt_bias), rep(-jnp.exp(m2_a_log.astype(F32))), rep(m2_d), row(m2_norm)
    tri = (jnp.arange(2 * SB_TILE)[:, None] % SB_TILE >= jnp.arange(SB_TILE)[None, :]).astype(BF16)
    bias_rows = jnp.pad(sb_beta_bias, ((0, 0), (0, 128 - SB_HEADS)))
    h0r = state_s5_re.reshape(depth, dec_b, 2048)
    h0i = state_s5_im.reshape(depth, dec_b, 2048)
    mem2 = mem_prompt.reshape(bsz * mt, D_MODEL)

    hp = x_prompt.reshape(bsz * sl, D_MODEL)
    hs = x_sample.reshape(dec_b, D_MODEL)
    outs = [[] for _ in range(14)]
    for l in range(depth):
        wcat, poutr, pouti, pw, bbr, bbi = _s5_prep(s5_lambda_re[l], s5_lambda_im[l], s5_log_dt[l],
                                                     s5_b_re[l], s5_b_im[l], s5_c_re[l], s5_c_im[l])
        a16r, a16i = pw[:, 0, S5_T].reshape(1, 2048), pw[:, 1, S5_T].reshape(1, 2048)
        a1r, a1i = pw[:, 0, 1].reshape(1, 2048), pw[:, 1, 1].reshape(1, 2048)
        bbr_d = _block_diag(jnp.swapaxes(bbr[:, :, :S5_CH], 1, 2)).astype(BF16)
        bbi_d = _block_diag(jnp.swapaxes(bbi[:, :, :S5_CH], 1, 2)).astype(BF16)
        ccr_d = _block_diag(jnp.swapaxes(s5_c_re[l], 1, 2)).astype(BF16)
        cci_d = _block_diag(jnp.swapaxes(s5_c_im[l], 1, 2)).astype(BF16)

        u, k, v, qb, kb, vb, zg, xbc, gates, dtr = _in_proj(hp, g_mix, w_all, l)
        r3 = lambda t: t.reshape(bsz, sl, t.shape[-1])
        y5, s5r, s5i = _s5_prompt(r3(u), wcat, poutr, pouti, a16r, a16i, s5d, wglu, bglu, l)
        ysb = _sb_prompt(r3(qb), r3(kb), r3(vb), sb_beta_bias[l], tri)
        ym, ssm, cvs = _ssd_prompt(r3(xbc), r3(dtr), r3(zg), m2_conv_w, cb, dtb, a_neg, m2d, m2n, l)
        r2 = lambda t: t.reshape(bsz * sl, -1)
        hp = _merge(hp, r2(y5), r2(ysb), r2(ym), gates, w5, wsb, wm, wo, l)
        mk, mv, mkb, mvb = _mem_kv(mem2, wk, wv, l)
        hp = _xattn_prompt(hp.reshape(bsz, sl, D_MODEL), g_mem, wq, mkb.reshape(bsz, mt, D_MODEL),
                           mvb.reshape(bsz, mt, D_MODEL), wmo, l).reshape(bsz * sl, D_MODEL)
        hp = _mlp(hp, g_mlp, w1, w2, l)
        for lst, val in zip(outs[:8], (
                k.reshape(bsz, sl, SB_HEADS, SB_HEAD_DIM), v.reshape(bsz, sl, SB_HEADS, SB_HEAD_DIM),
                s5r.reshape(bsz, S5_GROUPS, S5_STATE), s5i.reshape(bsz, S5_GROUPS, S5_STATE),
                ssm.reshape(bsz, M2_HEADS, M2_HEAD_DIM, M2_STATE), cvs,
                mk.reshape(bsz, mt, MEM_HEADS, MEM_HEAD_DIM), mv.reshape(bsz, mt, MEM_HEADS, MEM_HEAD_DIM))):
            lst.append(val)

        u, k, v, qb, kb, vb, zg, xbc, gates, dtr = _in_proj(hs, g_mix, w_all, l)
        y5, s5r, s5i = _s5_step(u, h0r, h0i, a1r, a1i, bbr_d, bbi_d, ccr_d, cci_d, s5d, wglu, bglu, l)
        ysb = _sb_decode(qb.astype(F32), bias_rows[l:l + 1], cache_sb_k, cache_sb_v, page_table, l)
        ym, ssm, cvs = _ssd_step(xbc, state_conv, dtr, zg, state_ssm, m2_conv_w, cb, dtb, a_neg, m2d, m2n, l)
        hs = _merge(hs, y5, ysb, ym, gates, w5, wsb, wm, wo, l)
        hs = _xattn_sample(hs, g_mem, wq, cache_mem_k, cache_mem_v, wmo, l)
        hs = _mlp(hs, g_mlp, w1, w2, l)
        for lst, val in zip(outs[8:], (
                k.reshape(dec_b, 1, SB_HEADS, SB_HEAD_DIM), v.reshape(dec_b, 1, SB_HEADS, SB_HEAD_DIM),
                s5r.reshape(dec_b, S5_GROUPS, S5_STATE), s5i.reshape(dec_b, S5_GROUPS, S5_STATE),
                ssm.reshape(dec_b, M2_HEADS, M2_HEAD_DIM, M2_STATE), cvs)):
            lst.append(val)

    g_fin = norm_final[None, :]
    y_prompt = _final_norm(hp, g_fin).reshape(bsz, sl, D_MODEL)
    y_sample = _final_norm(hs, g_fin).reshape(dec_b, 1, D_MODEL)
    return (y_prompt, y_sample) + tuple(jnp.stack(o) for o in outs)
```

```python
import functools
import math

import jax
import jax.numpy as jnp
from jax import lax
from jax.experimental import pallas as pl
from jax.experimental.pallas import tpu as pltpu

F32 = jnp.float32
BF16 = jnp.bfloat16

D_MODEL = 1024
PAGE_SIZE = 128
S5_CH = 16
S5_WIDTH = 512
S5_GROUPS = 32
S5_STATE = 64
S5_LBLK = 2048
S5_T = 16
SB_WIDTH = 512
SB_HEADS = 8
SB_HEAD_DIM = 64
SB_TILE = 256
M2_INNER = 512
M2_HEADS = 8
M2_HEAD_DIM = 64
M2_GROUPS = 2
M2_STATE = 128
M2_CONV = 4
M2_CONV_DIM = 1024
M2_Q = 256
MEM_HEADS = 4
MEM_HEAD_DIM = 256
D_FF = 4096
EPS = 1e-6
IN_MAIN = 3584
IN_DT = 3584
IN_GATE = 3592
VMEM_LIMIT_V7X = 56 * 1024 * 1024


def _cp(*sem):
    return pltpu.CompilerParams(dimension_semantics=sem, vmem_limit_bytes=VMEM_LIMIT_V7X)


def _tile(m, pref):
    t = min(m, pref)
    while m % t:
        t -= 8
    return t


def _full(shape):
    nd = len(shape)
    return pl.BlockSpec(shape, lambda *_: (0,) * nd)


def _layer(shape, l):
    nd = len(shape)
    return pl.BlockSpec((None,) + tuple(shape), lambda *_: (l,) + (0,) * nd)


def _rms(x, g):
    return x * lax.rsqrt(jnp.mean(x * x, axis=-1, keepdims=True) + EPS) * g


def _bdot(a, b):
    return jnp.dot(a.astype(BF16), b.astype(BF16), preferred_element_type=F32)


def _dot_nt(a, b):
    return lax.dot_general(a, b, (((1,), (1,)), ((), ())), preferred_element_type=F32)


def _dot_tn(a, b):
    return lax.dot_general(a, b, (((0,), (0,)), ((), ())), preferred_element_type=F32)


def _softplus(z):
    return jnp.maximum(z, 0.0) + jnp.log(1.0 + jnp.exp(-jnp.abs(z)))


def _sigmoid(z):
    return 1.0 / (1.0 + jnp.exp(-z))


def _silu(z):
    return z * _sigmoid(z)


def _gelu_tanh(x):
    return 0.5 * x * (1.0 + jnp.tanh(math.sqrt(2.0 / math.pi) * (x + 0.044715 * (x * x * x))))


def _in_proj_common(x_ref, g_ref, w_ref, zg_ref, xbc_ref, gate_ref, dt_ref):
    xn = _rms(x_ref[...], g_ref[...]).astype(BF16)

    def mm(lo, hi):
        return jnp.dot(xn, w_ref[:, lo:hi], preferred_element_type=F32)

    zg_ref[...] = mm(2048, 2560)
    xbc_ref[...] = mm(2560, 3584)
    gate_ref[...] = mm(3584, 6656)
    dt_ref[...] = mm(6656, 7168)
    return mm


def _in_proj_prompt_kernel(l, bias_ref, x_ref, g_ref, w_ref, u4_ref, kt_ref, vt_ref, q2_ref, k2_ref, vb_ref,
                           zg_ref, xbc_ref, gate_ref, dt_ref):
    mm = _in_proj_common(x_ref, g_ref, w_ref, zg_ref, xbc_ref, gate_ref, dt_ref)
    tm = x_ref.shape[0]
    u = mm(0, 512)
    for j in range(4):
        u4_ref[pl.ds(j, tm, stride=4), :] = u[:, j * 128:(j + 1) * 128]
    q = mm(512, 1024) * (SB_HEAD_DIM ** -0.5)
    k = mm(1024, 1536)
    v = mm(1536, 2048)
    kt_ref[...] = k.T
    vt_ref[...] = v.T
    vb_ref[...] = v.astype(BF16)
    lane = lax.broadcasted_iota(jnp.int32, (tm, 128), 1)
    for h in range(SB_HEADS):
        sl = slice((h // 2) * 128, (h // 2 + 1) * 128)
        qh, kh = q[:, sl], k[:, sl]
        if h % 2:
            qh, kh = pltpu.roll(qh, 64, 1), pltpu.roll(kh, 64, 1)
        extra = lane == SB_HEAD_DIM
        q2_ref[:, h * 128:(h + 1) * 128] = jnp.where(
            lane < SB_HEAD_DIM, qh, jnp.where(extra, bias_ref[l, h], 0.0)).astype(BF16)
        k2_ref[:, h * 128:(h + 1) * 128] = jnp.where(
            lane < SB_HEAD_DIM, kh, jnp.where(extra, 1.0, 0.0)).astype(BF16)


def _in_proj_prompt(x, g, w_all, bias, l, bsz):
    m = x.shape[0]
    sl = m // bsz
    tm = _tile(sl, 256)
    per_b = sl // tm
    row = lambda n: pl.BlockSpec((tm, n), lambda i: (i, 0))
    tr = pl.BlockSpec((None, 512, tm), lambda i: (i // per_b, 0, i % per_b))
    outs = [(512, F32), (1024, BF16), (1024, BF16), (512, BF16), (512, F32), (1024, F32), (3072, F32), (512, F32)]
    return pl.pallas_call(
        functools.partial(_in_proj_prompt_kernel, l),
        grid=(m // tm,),
        in_specs=[pl.BlockSpec(memory_space=pltpu.SMEM),
                  row(D_MODEL), _layer((1, D_MODEL), l), _layer((D_MODEL, 7168), l)],
        out_specs=[pl.BlockSpec((tm * 4, 128), lambda i: (i, 0)), tr, tr] + [row(n) for n, _ in outs[1:]],
        out_shape=[jax.ShapeDtypeStruct((m * 4, 128), F32),
                   jax.ShapeDtypeStruct((bsz, 512, sl), F32), jax.ShapeDtypeStruct((bsz, 512, sl), F32)]
                  + [jax.ShapeDtypeStruct((m, n), dt) for n, dt in outs[1:]],
        compiler_params=_cp("parallel"),
        name="in_proj",
    )(bias, x, g, w_all)


def _in_proj_sample_kernel(x_ref, g_ref, w_ref, u_ref, k_ref, v_ref, q_ref, zg_ref, xbc_ref, gate_ref, dt_ref):
    mm = _in_proj_common(x_ref, g_ref, w_ref, zg_ref, xbc_ref, gate_ref, dt_ref)
    u_ref[...] = mm(0, 512)
    q_ref[...] = mm(512, 1024) * (SB_HEAD_DIM ** -0.5)
    k_ref[...] = mm(1024, 1536)
    v_ref[...] = mm(1536, 2048)


def _in_proj_sample(x, g, w_all, l):
    m = x.shape[0]
    outs = [512, 512, 512, 512, 512, 1024, 3072, 512]
    return pl.pallas_call(
        _in_proj_sample_kernel,
        grid=(1,),
        in_specs=[_full((m, D_MODEL)), _layer((1, D_MODEL), l), _layer((D_MODEL, 7168), l)],
        out_specs=[_full((m, n)) for n in outs],
        out_shape=[jax.ShapeDtypeStruct((m, n), F32) for n in outs],
        compiler_params=_cp("arbitrary"),
        name="in_proj_step",
    )(x, g, w_all)


def _s5_prep_kernel(lr_row, li_row, lr_col, li_col, ldt_ref, btr_ref, bti_ref, ctr_ref, cti_ref,
                    wcat_ref, poutr_ref, pouti_ref, pw_ref, bbr_ref, bbi_ref):
    dt = jnp.exp(ldt_ref[0])
    lr, li = lr_row[0], li_row[0]
    kk = lax.broadcasted_iota(jnp.int32, (24, 1), 0).astype(F32)
    mag = jnp.exp(kk * (lr * dt))
    ang = kk * (li * dt)
    p_re, p_im = mag * jnp.cos(ang), mag * jnp.sin(ang)
    pw_ref[0, 0] = p_re
    pw_ref[0, 1] = p_im

    def rep(p, lo):
        return jnp.broadcast_to(p[lo:lo + S5_T][:, None, :], (S5_T, S5_CH, S5_STATE)).reshape(256, S5_STATE)

    c_re, c_im = ctr_ref[0], cti_ref[0]
    pr0, pi0 = rep(p_re, 0), rep(p_im, 0)
    l_re = c_re * pr0 - c_im * pi0
    l_im = c_re * pi0 + c_im * pr0
    pr1, pi1 = rep(p_re, 1), rep(p_im, 1)
    poutr_ref[0] = (c_re * pr1 - c_im * pi1).astype(BF16)
    pouti_ref[0] = (-(c_re * pi1 + c_im * pr1)).astype(BF16)

    lrc, lic = lr_col[0], li_col[0]
    m1 = jnp.exp(lrc * dt)
    a_re, a_im = m1 * jnp.cos(lic * dt), m1 * jnp.sin(lic * dt)
    den = lrc * lrc + lic * lic
    f_re = ((a_re - 1.0) * lrc + a_im * lic) / den
    f_im = (a_im * lrc - (a_re - 1.0) * lic) / den
    b_re, b_im = btr_ref[0], bti_ref[0]
    bb_re = f_re * b_re - f_im * b_im
    bb_im = f_re * b_im + f_im * b_re
    bbr_ref[0] = bb_re
    bbi_ref[0] = bb_im

    hp = lax.Precision.HIGHEST
    kt = (jnp.dot(l_re, bb_re, precision=hp, preferred_element_type=F32)
          - jnp.dot(l_im, bb_im, precision=hp, preferred_element_type=F32))
    lane_s = lax.broadcasted_iota(jnp.int32, (1, 256), 1) // S5_CH
    toep = jnp.where(lane_s == 0, kt, 0.0)
    for s in range(1, S5_T):
        sh = jnp.concatenate([jnp.zeros((S5_CH * s, 256), F32), kt[:256 - S5_CH * s]], axis=0)
        toep = jnp.where(lane_s == s, sh, toep)

    ks = (S5_T - 1 - lane_s).astype(F32)
    magc = jnp.exp(ks * (lrc * dt))
    angc = ks * (lic * dt)
    q_re, q_im = magc * jnp.cos(angc), magc * jnp.sin(angc)
    wcat_ref[0, 0:256] = toep.astype(BF16)
    wcat_ref[0, 256:320] = (q_re * bb_re - q_im * bb_im).astype(BF16)
    wcat_ref[0, 320:384] = (q_re * bb_im + q_im * bb_re).astype(BF16)


def _s5_prep(lam_re, lam_im, log_dt, b_re, b_im, c_re, c_im):
    g = S5_GROUPS
    per = lambda *shape: pl.BlockSpec((1,) + shape, lambda i: (i,) + (0,) * len(shape))
    tile_b = lambda b: jnp.tile(b, (1, 1, S5_T))
    tile_c = lambda c: jnp.tile(c, (1, S5_T, 1))
    return pl.pallas_call(
        _s5_prep_kernel,
        grid=(g,),
        in_specs=[per(1, 64), per(1, 64), per(64, 1), per(64, 1), per(1, 1),
                  per(64, 256), per(64, 256), per(256, 64), per(256, 64)],
        out_specs=[per(384, 256), per(256, 64), per(256, 64), per(2, 24, 64), per(64, 256), per(64, 256)],
        out_shape=[jax.ShapeDtypeStruct((g, 384, 256), BF16),
                   jax.ShapeDtypeStruct((g, 256, 64), BF16),
                   jax.ShapeDtypeStruct((g, 256, 64), BF16),
                   jax.ShapeDtypeStruct((g, 2, 24, 64), F32),
                   jax.ShapeDtypeStruct((g, 64, 256), F32),
                   jax.ShapeDtypeStruct((g, 64, 256), F32)],
        compiler_params=_cp("parallel"),
        name="s5_prep",
    )(lam_re[:, None, :], lam_im[:, None, :], lam_re[:, :, None], lam_im[:, :, None],
      log_dt[:, None, None], tile_b(b_re), tile_b(b_im), tile_c(c_re), tile_c(c_im))


def _rows(u_ref, s, j, n):
    return u_ref[0, pl.ds(s * 4 + j, n, stride=4 * S5_T), :]


def _s5_prompt_kernel(u_ref, wcat_ref, poutr_ref, pouti_ref, ar_ref, ai_ref, d_ref, wglu_ref, bglu_ref,
                      y_ref, sr_ref, si_ref, v_s, yt_s, hr_s, hi_s, cr_s, ci_s):
    n = v_s.shape[2]
    lb = pl.program_id(1)

    @pl.when(lb == 0)
    def _():
        cr_s[...] = jnp.zeros_like(cr_s)
        ci_s[...] = jnp.zeros_like(ci_s)

    for s in range(S5_T):
        for j in range(4):
            v_s[s, j * 128:(j + 1) * 128, :] = _rows(u_ref, s, j, n).T.astype(BF16)

    for gp in range(S5_GROUPS // 2):
        res = []
        for g in (2 * gp, 2 * gp + 1):
            vg = v_s[:, g * S5_CH:(g + 1) * S5_CH, :].reshape(S5_T * S5_CH, n)
            r = jnp.dot(wcat_ref[g], vg, preferred_element_type=F32)
            yt_s[g] = r[0:256]
            res.append(r)
        hr_s[:, gp * 128:(gp + 1) * 128] = jnp.concatenate([res[0][256:320], res[1][256:320]], axis=0).T
        hi_s[:, gp * 128:(gp + 1) * 128] = jnp.concatenate([res[0][320:384], res[1][320:384]], axis=0).T

    a_re, a_im = ar_ref[...], ai_ref[...]

    def step(i, carry):
        h_re, h_im = carry
        s_re = hr_s[pl.ds(i, 1), :]
        s_im = hi_s[pl.ds(i, 1), :]
        hr_s[pl.ds(i, 1), :] = h_re
        hi_s[pl.ds(i, 1), :] = h_im
        return (a_re * h_re - a_im * h_im + s_re, a_re * h_im + a_im * h_re + s_im)

    h_re, h_im = lax.fori_loop(0, n, step, (cr_s[...], ci_s[...]))
    cr_s[...] = h_re
    ci_s[...] = h_im
    sr_ref[0] = h_re
    si_ref[0] = h_im

    for gp in range(S5_GROUPS // 2):
        ht_re = hr_s[:, gp * 128:(gp + 1) * 128].T.astype(BF16)
        ht_im = hi_s[:, gp * 128:(gp + 1) * 128].T.astype(BF16)
        for j, g in enumerate((2 * gp, 2 * gp + 1)):
            yt_s[g] += (jnp.dot(poutr_ref[g], ht_re[64 * j:64 * j + 64], preferred_element_type=F32)
                        + jnp.dot(pouti_ref[g], ht_im[64 * j:64 * j + 64], preferred_element_type=F32))

    for t in range(S5_T):
        yt = yt_s[:, t * S5_CH:(t + 1) * S5_CH, :].reshape(S5_WIDTH, n).T
        y = yt + d_ref[...] * jnp.concatenate([_rows(u_ref, t, j, n) for j in range(4)], axis=1)
        gl = _gelu_tanh(y)
        o = gl * _sigmoid(jnp.dot(gl.astype(BF16), wglu_ref[...], preferred_element_type=F32) + bglu_ref[...])
        for j in range(4):
            y_ref[0, pl.ds(t * 4 + j, n, stride=4 * S5_T), :] = o[:, j * 128:(j + 1) * 128]


def _s5_prompt(u4, wcat, poutr, pouti, a16r, a16i, d, wglu, bglu, l):
    b, sl = u4.shape[0], u4.shape[1] // 4
    lblk = _tile(sl, S5_LBLK)
    n = lblk // S5_T
    return pl.pallas_call(
        _s5_prompt_kernel,
        grid=(b, sl // lblk),
        in_specs=[pl.BlockSpec((1, lblk * 4, 128), lambda i, j: (i, j, 0)),
                  _full((S5_GROUPS, 384, 256)), _full((S5_GROUPS, 256, 64)), _full((S5_GROUPS, 256, 64)),
                  _full((1, 2048)), _full((1, 2048)),
                  _layer((1, S5_WIDTH), l), _layer((S5_WIDTH, S5_WIDTH), l), _layer((1, S5_WIDTH), l)],
        out_specs=[pl.BlockSpec((1, lblk * 4, 128), lambda i, j: (i, j, 0)),
                   pl.BlockSpec((1, 1, 2048), lambda i, j: (i, 0, 0)),
                   pl.BlockSpec((1, 1, 2048), lambda i, j: (i, 0, 0))],
        out_shape=[jax.ShapeDtypeStruct((b, sl * 4, 128), F32),
                   jax.ShapeDtypeStruct((b, 1, 2048), F32),
                   jax.ShapeDtypeStruct((b, 1, 2048), F32)],
        scratch_shapes=[pltpu.VMEM((S5_T, S5_WIDTH, n), BF16),
                        pltpu.VMEM((S5_GROUPS, 256, n), F32),
                        pltpu.VMEM((n, 2048), F32), pltpu.VMEM((n, 2048), F32),
                        pltpu.VMEM((1, 2048), F32), pltpu.VMEM((1, 2048), F32)],
        compiler_params=_cp("parallel", "arbitrary"),
        name="s5_prompt",
    )(u4, wcat, poutr, pouti, a16r, a16i, d, wglu, bglu)


def _s5_step_kernel(u_ref, h0r_ref, h0i_ref, ar_ref, ai_ref, bbr_ref, bbi_ref, ccr_ref, cci_ref,
                    d_ref, wglu_ref, bglu_ref, y_ref, hr_ref, hi_ref):
    u = u_ref[...]
    ub = u.astype(BF16)
    a_re, a_im = ar_ref[...], ai_ref[...]
    h0r, h0i = h0r_ref[...], h0i_ref[...]
    h_re = a_re * h0r - a_im * h0i + jnp.dot(ub, bbr_ref[...], preferred_element_type=F32)
    h_im = a_re * h0i + a_im * h0r + jnp.dot(ub, bbi_ref[...], preferred_element_type=F32)
    hr_ref[...] = h_re
    hi_ref[...] = h_im
    y = (jnp.dot(h_re.astype(BF16), ccr_ref[...], preferred_element_type=F32)
         - jnp.dot(h_im.astype(BF16), cci_ref[...], preferred_element_type=F32) + d_ref[...] * u)
    gl = _gelu_tanh(y)
    y_ref[...] = gl * _sigmoid(jnp.dot(gl.astype(BF16), wglu_ref[...], preferred_element_type=F32) + bglu_ref[...])


def _s5_step(u, h0r, h0i, a1r, a1i, bbr, bbi, ccr, cci, d, wglu, bglu, l):
    m = u.shape[0]
    return pl.pallas_call(
        _s5_step_kernel,
        grid=(1,),
        in_specs=[_full((m, S5_WIDTH)), _layer((m, 2048), l), _layer((m, 2048), l),
                  _full((1, 2048)), _full((1, 2048)),
                  _full((S5_WIDTH, 2048)), _full((S5_WIDTH, 2048)), _full((2048, S5_WIDTH)), _full((2048, S5_WIDTH)),
                  _layer((1, S5_WIDTH), l), _layer((S5_WIDTH, S5_WIDTH), l), _layer((1, S5_WIDTH), l)],
        out_specs=[_full((m, S5_WIDTH)), _full((m, 2048)), _full((m, 2048))],
        out_shape=[jax.ShapeDtypeStruct((m, S5_WIDTH), F32),
                   jax.ShapeDtypeStruct((m, 2048), F32), jax.ShapeDtypeStruct((m, 2048), F32)],
        compiler_params=_cp("arbitrary"),
        name="s5_step",
    )(u, h0r, h0i, a1r, a1i, bbr, bbi, ccr, cci, d, wglu, bglu)


def _sb_prompt_kernel(q_ref, k_ref, v_ref, tri_ref, o_ref):
    tq, tk = q_ref.shape[1], SB_TILE
    per = tq // tk
    qi = pl.program_id(2)
    qs = (q_ref[0, :, 0:128], q_ref[0, :, 128:256])
    lane = lax.broadcasted_iota(jnp.int32, (tq, 128), 1)
    row = lax.broadcasted_iota(jnp.int32, (tq, tk), 0)
    col = lax.broadcasted_iota(jnp.int32, (tq, tk), 1)

    def block(kb, carry, off):
        start = pl.multiple_of(kb * tk, tk)
        kblk = k_ref[0, pl.ds(start, tk), :]
        vblk = v_ref[0, pl.ds(start, tk), :]
        new = []
        for hh in range(2):
            acc, c = carry[hh]
            z = _dot_nt(qs[hh], kblk[:, hh * 128:(hh + 1) * 128])
            sp = _softplus(z)
            if off is not None:
                valid = col + off < row
                sp = jnp.where(valid, sp, 0.0)
            r = jnp.dot(sp.astype(BF16), tri_ref[...], preferred_element_type=F32)
            w = jnp.exp(z - r - c)
            if off is not None:
                w = jnp.where(valid, w, 0.0)
            acc = acc + jnp.dot(w.astype(BF16), vblk, preferred_element_type=F32)
            new.append((acc, c + r[:, 0:1]))
        return tuple(new)

    zero = (jnp.zeros((tq, 128), F32), jnp.zeros((tq, 1), F32))
    carry = (zero, zero)
    for d in range(per - 1, -1, -1):
        carry = block(qi * per + d, carry, d * tk)
    carry = lax.fori_loop(0, qi * per, lambda j, cr: block(qi * per - 1 - j, cr, None), carry)
    o_ref[0] = jnp.where(lane < SB_HEAD_DIM, carry[0][0], carry[1][0])


def _sb_prompt(q2, k2, vb, tri):
    b, sl, _ = vb.shape
    tq = _tile(sl, 2 * SB_TILE)
    return pl.pallas_call(
        _sb_prompt_kernel,
        grid=(b, SB_HEADS // 2, sl // tq),
        in_specs=[pl.BlockSpec((1, tq, 256), lambda i, p, j: (i, j, p)),
                  pl.BlockSpec((1, sl, 256), lambda i, p, j: (i, 0, p)),
                  pl.BlockSpec((1, sl, 128), lambda i, p, j: (i, 0, p)),
                  _full((SB_TILE, SB_TILE))],
        out_specs=pl.BlockSpec((1, tq, 128), lambda i, p, j: (i, j, p)),
        out_shape=jax.ShapeDtypeStruct((b, sl, SB_WIDTH), F32),
        compiler_params=_cp("parallel", "parallel", "arbitrary"),
        name="sb_prompt",
    )(q2, k2, vb, tri)


def _split_bf16(x):
    hi = x.astype(BF16)
    return hi, (x - hi.astype(F32)).astype(BF16)


def _sb_decode_kernel(npg, pt_ref, q_ref, bias_ref, tri_ref, later_ref, *refs):
    k_refs, v_refs = refs[:npg], refs[npg:2 * npg]
    o_ref, qc_s, acc_s, c_s = refs[2 * npg:]
    j = pl.program_id(1)

    @pl.when(j == 0)
    def _():
        qc_s[...] = jnp.broadcast_to(q_ref[0], (128, SB_WIDTH)).T
        acc_s[...] = jnp.zeros_like(acc_s)
        c_s[...] = jnp.zeros_like(c_s)

    qc = qc_s[...]
    bias = jnp.concatenate([bias_ref[...]] * npg, axis=0)
    z = jnp.concatenate(
        [(k_refs[i][...] * qc).reshape(SB_HEADS, SB_HEAD_DIM, PAGE_SIZE).sum(axis=1) for i in range(npg)],
        axis=0) + bias
    lk = -_softplus(z)
    hi, lo = _split_bf16(lk)
    r_loc = jnp.dot(jnp.concatenate([hi, lo], axis=1), tri_ref[...], preferred_element_type=F32)
    tot = jnp.broadcast_to(r_loc[:, 0:1], r_loc.shape)
    hi, lo = _split_bf16(tot)
    c_in = c_s[...]
    r = (r_loc + jnp.dot(later_ref[...], jnp.concatenate([hi, lo], axis=0), preferred_element_type=F32)
         + jnp.concatenate([c_in] * npg, axis=0))
    w = jnp.exp(z + r)
    c_s[...] = r[0:SB_HEADS] - r_loc[0:SB_HEADS] + tot[0:SB_HEADS]
    acc = acc_s[...]
    for i in range(npg):
        w_rows = jnp.broadcast_to(w[i * SB_HEADS:(i + 1) * SB_HEADS][:, None, :],
                                  (SB_HEADS, SB_HEAD_DIM, PAGE_SIZE)).reshape(SB_WIDTH, PAGE_SIZE)
        acc = acc + v_refs[i][...] * w_rows
    acc_s[...] = acc

    @pl.when(j == pl.num_programs(1) - 1)
    def _():
        o_ref[0] = jnp.sum(acc.T, axis=0, keepdims=True)


def _sb_decode(q, bias_col, cache_k, cache_v, page_table, l):
    m = q.shape[0]
    n_pages = page_table.shape[1]
    npg = 16
    while n_pages % npg:
        npg //= 2
    steps = n_pages // npg
    rows = npg * SB_HEADS

    def page(i):
        return pl.BlockSpec((None, None, SB_WIDTH, PAGE_SIZE),
                            lambda b, j, pt: (l, pt[b, (steps - 1 - j) * npg + i], 0, 0))

    const = lambda shape: pl.BlockSpec(shape, lambda b, j, pt: (0,) * len(shape))
    grid_spec = pltpu.PrefetchScalarGridSpec(
        num_scalar_prefetch=1,
        grid=(m, steps),
        in_specs=[pl.BlockSpec((1, 1, SB_WIDTH), lambda b, j, pt: (b, 0, 0)),
                  pl.BlockSpec((None, SB_HEADS, 128), lambda b, j, pt: (l, 0, 0)),
                  const((2 * PAGE_SIZE, PAGE_SIZE)), const((rows, 2 * rows))]
                 + [page(i) for i in range(npg)] * 2,
        out_specs=pl.BlockSpec((1, 1, SB_WIDTH), lambda b, j, pt: (b, 0, 0)),
        scratch_shapes=[pltpu.VMEM((SB_WIDTH, 128), F32), pltpu.VMEM((SB_WIDTH, PAGE_SIZE), F32),
                        pltpu.VMEM((SB_HEADS, 128), F32)],
    )
    depth, n_pool = cache_k.shape[:2]
    ck = jnp.transpose(cache_k, (0, 1, 3, 4, 2)).reshape(depth, n_pool, SB_WIDTH, PAGE_SIZE)
    cv = jnp.transpose(cache_v, (0, 1, 3, 4, 2)).reshape(depth, n_pool, SB_WIDTH, PAGE_SIZE)
    tri = (jnp.arange(2 * PAGE_SIZE)[:, None] % PAGE_SIZE >= jnp.arange(PAGE_SIZE)[None, :]).astype(BF16)
    ridx = jnp.arange(rows)
    later = ((ridx[None, :] % SB_HEADS == ridx[:, None] % SB_HEADS)
             & (ridx[None, :] // SB_HEADS > ridx[:, None] // SB_HEADS))
    later = jnp.concatenate([later, later], axis=1).astype(BF16)
    out = pl.pallas_call(
        functools.partial(_sb_decode_kernel, npg),
        grid_spec=grid_spec,
        out_shape=jax.ShapeDtypeStruct((m, 1, SB_WIDTH), F32),
        compiler_params=_cp("parallel", "arbitrary"),
        name="sb_decode",
    )(page_table, q.reshape(m, 1, SB_WIDTH), bias_col, tri, later, *([ck] * npg), *([cv] * npg))
    return out.reshape(m, SB_WIDTH)


def _cumsum_rows(x):
    n = x.shape[0]
    row = lax.broadcasted_iota(jnp.int32, (n, 1), 0)
    k = 1
    while k < n:
        x = x + jnp.where(row >= k, pltpu.roll(x, k, 0), 0.0)
        k *= 2
    return x


def _ssd_prompt_kernel(xbc_ref, dt_ref, zg_ref, cw_ref, cb_ref, dtb_ref, a_ref, d_ref, nrm_ref,
                       y_ref, st_ref, cv_ref, prev_s, st_s):
    q = xbc_ref.shape[1]
    c = pl.program_id(1)

    @pl.when(c == 0)
    def _():
        prev_s[...] = jnp.zeros_like(prev_s)
        st_s[...] = jnp.zeros_like(st_s)

    u = xbc_ref[0]
    prev = prev_s[...]
    row8 = lax.broadcasted_iota(jnp.int32, (8, 1), 0)
    conv = cb_ref[...] + cw_ref[M2_CONV - 1:M2_CONV, :] * u
    for k in range(1, M2_CONV):
        ru = pltpu.roll(u, k, 0)
        top = jnp.where(row8 < k, pltpu.roll(prev, k, 0), ru[0:8])
        conv = conv + cw_ref[M2_CONV - 1 - k:M2_CONV - k, :] * jnp.concatenate([top, ru[8:]], axis=0)
    prev_s[...] = u[q - 8:q]
    xc = _silu(conv)
    xs = xc[:, 0:M2_INNER]
    dt = _softplus(dt_ref[0] + dtb_ref[...])
    acum = _cumsum_rows(dt * a_ref[...])
    a_last = acum[q - 1:q, :]
    xdt = xs * dt
    xw = (xdt * jnp.exp(a_last - acum)).astype(BF16)
    xdt_b = xdt.astype(BF16)
    e_acum = jnp.exp(acum)
    rows = lax.broadcasted_iota(jnp.int32, (q, q), 0)
    cols = lax.broadcasted_iota(jnp.int32, (q, q), 1)
    causal = cols <= rows
    lane = lax.broadcasted_iota(jnp.int32, (q, 128), 1)
    y_parts = []
    for g in range(M2_GROUPS):
        bm = xc[:, M2_INNER + g * M2_STATE:M2_INNER + (g + 1) * M2_STATE].astype(BF16)
        cm = xc[:, M2_INNER + (M2_GROUPS + g) * M2_STATE:M2_INNER + (M2_GROUPS + g + 1) * M2_STATE].astype(BF16)
        cb = _dot_nt(cm, bm)
        st = st_s[g]
        y_off = jnp.dot(cm, st.astype(BF16), preferred_element_type=F32) * e_acum[:, g * 256:(g + 1) * 256]
        st_s[g] = st * jnp.exp(a_last[:, g * 256:(g + 1) * 256]) + _dot_tn(bm, xw[:, g * 256:(g + 1) * 256])
        for pp in range(2):
            p = 2 * g + pp
            a_pair = acum[:, p * 128:(p + 1) * 128]
            a_t = a_pair.T
            yd = []
            for hh in range(2):
                seg = a_pair[:, 64 * hh:64 * hh + 1] - a_t[64 * hh:64 * hh + 1, :]
                mix = jnp.where(causal, cb * jnp.exp(jnp.minimum(seg, 0.0)), 0.0).astype(BF16)
                yd.append(jnp.dot(mix, xdt_b[:, p * 128:(p + 1) * 128], preferred_element_type=F32))
            y_parts.append(jnp.where(lane < M2_HEAD_DIM, yd[0], yd[1]) + y_off[:, pp * 128:(pp + 1) * 128])
    y = jnp.concatenate(y_parts, axis=1) + d_ref[...] * xs
    y_ref[0] = _rms(y * _silu(zg_ref[0]), nrm_ref[...])

    @pl.when(c == pl.num_programs(1) - 1)
    def _():
        for g in range(M2_GROUPS):
            st_ref[0, g * 256:(g + 1) * 256, :] = st_s[g].T
        cv_ref[0] = xbc_ref[0, q - (M2_CONV - 1):q, :]


def _ssd_prompt(xbc, dt_raw, zg, cw, cb, dtb, a, d, nrm, l):
    b, sl, _ = xbc.shape
    q = _tile(sl, M2_Q)
    tok = lambda n: pl.BlockSpec((1, q, n), lambda i, j: (i, j, 0))
    return pl.pallas_call(
        _ssd_prompt_kernel,
        grid=(b, sl // q),
        in_specs=[tok(M2_CONV_DIM), tok(M2_INNER), tok(M2_INNER),
                  _layer((M2_CONV, M2_CONV_DIM), l), _layer((1, M2_CONV_DIM), l),
                  _layer((1, M2_INNER), l), _layer((1, M2_INNER), l), _layer((1, M2_INNER), l),
                  _layer((1, M2_INNER), l)],
        out_specs=[tok(M2_INNER),
                   pl.BlockSpec((1, M2_INNER, M2_STATE), lambda i, j: (i, 0, 0)),
                   pl.BlockSpec((1, M2_CONV - 1, M2_CONV_DIM), lambda i, j: (i, 0, 0))],
        out_shape=[jax.ShapeDtypeStruct((b, sl, M2_INNER), F32),
                   jax.ShapeDtypeStruct((b, M2_INNER, M2_STATE), F32),
                   jax.ShapeDtypeStruct((b, M2_CONV - 1, M2_CONV_DIM), F32)],
        scratch_shapes=[pltpu.VMEM((8, M2_CONV_DIM), F32), pltpu.VMEM((M2_GROUPS, M2_STATE, 256), F32)],
        compiler_params=_cp("parallel", "arbitrary"),
        name="ssd_prompt",
    )(xbc, dt_raw, zg, cw, cb, dtb, a, d, nrm)


def _ssd_step_kernel(xbc_ref, cs_ref, dt_ref, zg_ref, st_ref, cw_ref, cb_ref, dtb_ref, a_ref, d_ref, nrm_ref,
                     y_ref, sto_ref, cvo_ref):
    nb = xbc_ref.shape[0]
    x = xbc_ref[...]
    b0, b1, b2 = cs_ref[:, 0, :], cs_ref[:, 1, :], cs_ref[:, 2, :]
    conv = (cb_ref[...] + cw_ref[0:1, :] * b0 + cw_ref[1:2, :] * b1 + cw_ref[2:3, :] * b2 + cw_ref[3:4, :] * x)
    cvo_ref[:, 0, :] = b1
    cvo_ref[:, 1, :] = b2
    cvo_ref[:, 2, :] = x
    xc = _silu(conv)
    xs = xc[:, 0:M2_INNER]
    dt = _softplus(dt_ref[...] + dtb_ref[...])
    dec = jnp.exp(dt * a_ref[...])
    pad = jnp.zeros((128 - nb, M2_INNER), F32)
    xdt_t = jnp.concatenate([xs * dt, pad], axis=0).T
    dec_t = jnp.concatenate([dec, pad], axis=0).T
    hrow = lax.broadcasted_iota(jnp.int32, (M2_INNER, 1), 0)
    lane = lax.broadcasted_iota(jnp.int32, (1, 128), 1)
    ycols = jnp.zeros((M2_INNER, 128), F32)
    for b in range(nb):
        brow = jnp.where(hrow < 256, xc[b:b + 1, 512:640], xc[b:b + 1, 640:768])
        crow = jnp.where(hrow < 256, xc[b:b + 1, 768:896], xc[b:b + 1, 896:1024])
        h_new = dec_t[:, b:b + 1] * st_ref[b] + xdt_t[:, b:b + 1] * brow
        sto_ref[b] = h_new
        ycols = jnp.where(lane == b, jnp.sum(h_new * crow, axis=1, keepdims=True), ycols)
    y = ycols.T[0:nb] + d_ref[...] * xs
    y_ref[...] = _rms(y * _silu(zg_ref[...]), nrm_ref[...])


def _ssd_step(xbc, conv_state, dt_raw, zg, ssm_state, cw, cb, dtb, a, d, nrm, l):
    m = xbc.shape[0]
    nb = 8
    tok = lambda n: pl.BlockSpec((nb, n), lambda i: (i, 0))
    depth = ssm_state.shape[0]
    st = ssm_state.reshape(depth, m, M2_INNER, M2_STATE)
    return pl.pallas_call(
        _ssd_step_kernel,
        grid=(m // nb,),
        in_specs=[tok(M2_CONV_DIM),
                  pl.BlockSpec((None, nb, M2_CONV - 1, M2_CONV_DIM), lambda i: (l, i, 0, 0)),
                  tok(M2_INNER), tok(M2_INNER),
                  pl.BlockSpec((None, nb, M2_INNER, M2_STATE), lambda i: (l, i, 0, 0)),
                  _layer((M2_CONV, M2_CONV_DIM), l), _layer((1, M2_CONV_DIM), l),
                  _layer((1, M2_INNER), l), _layer((1, M2_INNER), l), _layer((1, M2_INNER), l),
                  _layer((1, M2_INNER), l)],
        out_specs=[tok(M2_INNER),
                   pl.BlockSpec((nb, M2_INNER, M2_STATE), lambda i: (i, 0, 0)),
                   pl.BlockSpec((nb, M2_CONV - 1, M2_CONV_DIM), lambda i: (i, 0, 0))],
        out_shape=[jax.ShapeDtypeStruct((m, M2_INNER), F32),
                   jax.ShapeDtypeStruct((m, M2_INNER, M2_STATE), F32),
                   jax.ShapeDtypeStruct((m, M2_CONV - 1, M2_CONV_DIM), F32)],
        compiler_params=_cp("parallel"),
        name="ssd_step",
    )(xbc, conv_state, dt_raw, zg, st, cw, cb, dtb, a, d, nrm)


def _merge_kernel(split5, h_ref, y5_ref, ysb_ref, ym_ref, gate_ref, w5_ref, wsb_ref, wm_ref, wo_ref, o_ref):
    tm = h_ref.shape[0]

    def branch(y, w_ref, lo):
        return _sigmoid(gate_ref[:, lo:lo + D_MODEL]) * jnp.dot(
            y.astype(BF16), w_ref[...], preferred_element_type=F32)

    if split5:
        y5 = jnp.concatenate([y5_ref[pl.ds(j, tm, stride=4), :] for j in range(4)], axis=1)
    else:
        y5 = y5_ref[...]
    merged = (branch(y5, w5_ref, 0) + branch(ysb_ref[...], wsb_ref, D_MODEL)
              + branch(ym_ref[...], wm_ref, 2 * D_MODEL))
    o_ref[...] = h_ref[...] + jnp.dot(merged.astype(BF16), wo_ref[...], preferred_element_type=F32)


def _merge(h, y5, ysb, ym, gates, w5, wsb, wm, wo, l):
    m = h.shape[0]
    tm = _tile(m, 512)
    row = lambda n: pl.BlockSpec((tm, n), lambda i: (i, 0))
    split5 = y5.shape[-1] == 128
    y5_spec = pl.BlockSpec((tm * 4, 128), lambda i: (i, 0)) if split5 else row(512)
    return pl.pallas_call(
        functools.partial(_merge_kernel, split5),
        grid=(m // tm,),
        in_specs=[row(D_MODEL), y5_spec, row(512), row(512), row(3 * D_MODEL),
                  _layer((512, D_MODEL), l), _layer((512, D_MODEL), l), _layer((512, D_MODEL), l),
                  _layer((D_MODEL, D_MODEL), l)],
        out_specs=row(D_MODEL),
        out_shape=jax.ShapeDtypeStruct((m, D_MODEL), F32),
        compiler_params=_cp("parallel"),
        name="merge",
    )(h, y5, ysb, ym, gates, w5, wsb, wm, wo)


def _mem_kv_kernel(x_ref, wk_ref, wv_ref, k_ref, v_ref, kb_ref, vb_ref):
    xb = x_ref[...].astype(BF16)
    k = jnp.dot(xb, wk_ref[...], preferred_element_type=F32)
    v = jnp.dot(xb, wv_ref[...], preferred_element_type=F32)
    k_ref[...] = k
    v_ref[...] = v
    kb_ref[...] = k.astype(BF16)
    vb_ref[...] = v.astype(BF16)


def _mem_kv(mem, wk, wv, l):
    m = mem.shape[0]
    tm = _tile(m, 512)
    row = pl.BlockSpec((tm, D_MODEL), lambda i: (i, 0))
    return pl.pallas_call(
        _mem_kv_kernel,
        grid=(m // tm,),
        in_specs=[row, _layer((D_MODEL, D_MODEL), l), _layer((D_MODEL, D_MODEL), l)],
        out_specs=[row] * 4,
        out_shape=[jax.ShapeDtypeStruct((m, D_MODEL), F32)] * 2 + [jax.ShapeDtypeStruct((m, D_MODEL), BF16)] * 2,
        compiler_params=_cp("parallel"),
        name="mem_kv",
    )(mem, wk, wv)


def _xattn_prompt_kernel(h_ref, g_ref, wq_ref, mk_ref, mv_ref, wo_ref, o_ref):
    h = h_ref[0]
    xn = _rms(h, g_ref[...]).astype(BF16)
    q = (jnp.dot(xn, wq_ref[...], preferred_element_type=F32) * (MEM_HEAD_DIM ** -0.5)).astype(BF16)
    heads = []
    for hd in range(MEM_HEADS):
        sl = slice(hd * MEM_HEAD_DIM, (hd + 1) * MEM_HEAD_DIM)
        s = _dot_nt(q[:, sl], mk_ref[0, :, sl])
        e = jnp.exp(s - jnp.max(s, axis=-1, keepdims=True))
        p = e / jnp.sum(e, axis=-1, keepdims=True)
        heads.append(jnp.dot(p.astype(BF16), mv_ref[0, :, sl], preferred_element_type=F32))
    o = jnp.concatenate(heads, axis=1).astype(BF16)
    o_ref[0] = h + jnp.dot(o, wo_ref[...], preferred_element_type=F32)


def _xattn_prompt(h, g, wq, mkb, mvb, wo, l):
    b, sl, _ = h.shape
    tq = _tile(sl, 512)
    mt = mkb.shape[1]
    return pl.pallas_call(
        _xattn_prompt_kernel,
        grid=(b, sl // tq),
        in_specs=[pl.BlockSpec((1, tq, D_MODEL), lambda i, j: (i, j, 0)),
                  _layer((1, D_MODEL), l), _layer((D_MODEL, D_MODEL), l),
                  pl.BlockSpec((1, mt, D_MODEL), lambda i, j: (i, 0, 0)),
                  pl.BlockSpec((1, mt, D_MODEL), lambda i, j: (i, 0, 0)),
                  _layer((D_MODEL, D_MODEL), l)],
        out_specs=pl.BlockSpec((1, tq, D_MODEL), lambda i, j: (i, j, 0)),
        out_shape=jax.ShapeDtypeStruct((b, sl, D_MODEL), F32),
        compiler_params=_cp("parallel", "parallel"),
        name="xattn_prompt",
    )(h, g, wq, mkb, mvb, wo)


def _xattn_sample_kernel(h_ref, g_ref, wq_ref, mk_ref, mv_ref, wo_ref, o_ref, q_s, att_s):
    b = pl.program_id(0)

    @pl.when(b == 0)
    def _():
        xn = _rms(h_ref[...], g_ref[...]).astype(BF16)
        q_s[...] = jnp.dot(xn, wq_ref[...], preferred_element_type=F32) * (MEM_HEAD_DIM ** -0.5)

    head_of_row = lax.broadcasted_iota(jnp.int32, (D_MODEL, 128), 0) // MEM_HEAD_DIM
    e_mat = (head_of_row == lax.broadcasted_iota(jnp.int32, (D_MODEL, 128), 1)).astype(BF16)
    head_of_col = lax.broadcasted_iota(jnp.int32, (128, D_MODEL), 1) // MEM_HEAD_DIM
    e_t = (head_of_col == lax.broadcasted_iota(jnp.int32, (128, D_MODEL), 0)).astype(BF16)
    q = q_s[pl.ds(b, 1), :]
    s = jnp.dot((mk_ref[...] * q).astype(BF16), e_mat, preferred_element_type=F32)
    e = jnp.exp(s - jnp.max(s, axis=0, keepdims=True))
    p = e / jnp.sum(e, axis=0, keepdims=True)
    pv = jnp.dot(p.astype(BF16), e_t, preferred_element_type=F32) * mv_ref[...]
    att_s[pl.ds(b, 1), :] = jnp.sum(pv, axis=0, keepdims=True)

    @pl.when(b == pl.num_programs(0) - 1)
    def _():
        o_ref[...] = h_ref[...] + jnp.dot(att_s[...].astype(BF16), wo_ref[...], preferred_element_type=F32)


def _xattn_sample(h, g, wq, cache_k, cache_v, wo, l):
    m = h.shape[0]
    depth, _, mt = cache_k.shape[:3]
    ck = cache_k.reshape(depth, m, mt, D_MODEL)
    cv = cache_v.reshape(depth, m, mt, D_MODEL)
    kv = pl.BlockSpec((None, None, mt, D_MODEL), lambda i: (l, i, 0, 0))
    return pl.pallas_call(
        _xattn_sample_kernel,
        grid=(m,),
        in_specs=[_full((m, D_MODEL)), _layer((1, D_MODEL), l), _layer((D_MODEL, D_MODEL), l), kv, kv,
                  _layer((D_MODEL, D_MODEL), l)],
        out_specs=_full((m, D_MODEL)),
        out_shape=jax.ShapeDtypeStruct((m, D_MODEL), F32),
        scratch_shapes=[pltpu.VMEM((m, D_MODEL), F32), pltpu.VMEM((m, D_MODEL), F32)],
        compiler_params=_cp("arbitrary"),
        name="xattn_sample",
    )(h, g, wq, ck, cv, wo)


def _mlp_kernel(h_ref, g_ref, w1_ref, w2_ref, o_ref):
    h = h_ref[...]
    xn = _rms(h, g_ref[...]).astype(BF16)
    acc = h
    for j in range(D_FF // 1024):
        a = jnp.maximum(jnp.dot(xn, w1_ref[:, j * 1024:(j + 1) * 1024], preferred_element_type=F32), 0.0)
        acc = acc + jnp.dot((a * a).astype(BF16), w2_ref[j * 1024:(j + 1) * 1024, :], preferred_element_type=F32)
    o_ref[...] = acc


def _mlp(h, g, w1, w2, l):
    m = h.shape[0]
    tm = _tile(m, 512)
    row = pl.BlockSpec((tm, D_MODEL), lambda i: (i, 0))
    return pl.pallas_call(
        _mlp_kernel,
        grid=(m // tm,),
        in_specs=[row, _layer((1, D_MODEL), l), _layer((D_MODEL, D_FF), l), _layer((D_FF, D_MODEL), l)],
        out_specs=row,
        out_shape=jax.ShapeDtypeStruct((m, D_MODEL), F32),
        compiler_params=_cp("parallel"),
        name="mlp",
    )(h, g, w1, w2)


def _norm_kernel(h_ref, g_ref, o_ref):
    o_ref[...] = _rms(h_ref[...], g_ref[...])


def _final_norm(h, g):
    m = h.shape[0]
    tm = _tile(m, 1024)
    row = pl.BlockSpec((tm, D_MODEL), lambda i: (i, 0))
    return pl.pallas_call(
        _norm_kernel,
        grid=(m // tm,),
        in_specs=[row, _full((1, D_MODEL))],
        out_specs=row,
        out_shape=jax.ShapeDtypeStruct((m, D_MODEL), F32),
        compiler_params=_cp("parallel"),
        name="final_norm",
    )(h, g)


def _block_diag(blocks):
    g, r, c = blocks.shape
    eye = jnp.eye(g, dtype=blocks.dtype)
    return (blocks[:, :, None, :] * eye[:, None, :, None]).reshape(g * r, g * c)


def kernel(x_prompt, x_sample, mem_prompt, cache_sb_k, cache_sb_v, state_s5_re, state_s5_im, state_ssm, state_conv, cache_mem_k, cache_mem_v, page_table, norm_mix, w_in, s5_lambda_re, s5_lambda_im, s5_log_dt, s5_b_re, s5_b_im, s5_c_re, s5_c_im, s5_d, s5_w_glu, s5_b_glu, sb_beta_bias, m2_conv_w, m2_conv_b, m2_dt_bias, m2_a_log, m2_d, m2_norm, w_br_s5, w_br_sb, w_br_m2, w_out, norm_mem, mem_wq, mem_wk, mem_wv, mem_wo, norm_mlp, mlp_w1, mlp_w2, norm_final):
    depth = w_in.shape[0]
    bsz, sl, _ = x_prompt.shape
    dec_b = x_sample.shape[0]
    mt = mem_prompt.shape[1]
    rep = lambda p: jnp.repeat(p, M2_HEAD_DIM, axis=-1)[:, None, :]
    row = lambda p: p[:, None, :]

    w_all = jnp.concatenate([w_in[:, :, :IN_MAIN], w_in[:, :, IN_GATE:],
                             jnp.repeat(w_in[:, :, IN_DT:IN_GATE], M2_HEAD_DIM, axis=-1)], axis=-1).astype(BF16)
    wglu, w5, wsb, wm, wo = (w.astype(BF16) for w in (s5_w_glu, w_br_s5, w_br_sb, w_br_m2, w_out))
    wq, wk, wv, wmo, w1, w2 = (w.astype(BF16) for w in (mem_wq, mem_wk, mem_wv, mem_wo, mlp_w1, mlp_w2))
    g_mix, g_mem, g_mlp = row(norm_mix), row(norm_mem), row(norm_mlp)
    s5d, bglu, cb = row(s5_d), row(s5_b_glu), row(m2_conv_b)
    dtb, a_neg, m2d, m2n = rep(m2_dt_bias), rep(-jnp.exp(m2_a_log.astype(F32))), rep(m2_d), row(m2_norm)
    tri = (jnp.arange(SB_TILE)[:, None] >= jnp.arange(SB_TILE)[None, :]).astype(BF16)
    bias_col = jnp.broadcast_to(sb_beta_bias[:, :, None], (depth, SB_HEADS, 128))
    h0r = state_s5_re.reshape(depth, dec_b, 2048)
    h0i = state_s5_im.reshape(depth, dec_b, 2048)
    mem2 = mem_prompt.reshape(bsz * mt, D_MODEL)

    hp = x_prompt.reshape(bsz * sl, D_MODEL)
    hs = x_sample.reshape(dec_b, D_MODEL)
    outs = [[] for _ in range(14)]
    for l in range(depth):
        wcat, poutr, pouti, pw, bbr, bbi = _s5_prep(s5_lambda_re[l], s5_lambda_im[l], s5_log_dt[l],
                                                     s5_b_re[l], s5_b_im[l], s5_c_re[l], s5_c_im[l])
        a16r, a16i = pw[:, 0, S5_T].reshape(1, 2048), pw[:, 1, S5_T].reshape(1, 2048)
        a1r, a1i = pw[:, 0, 1].reshape(1, 2048), pw[:, 1, 1].reshape(1, 2048)
        bbr_d = _block_diag(jnp.swapaxes(bbr[:, :, :S5_CH], 1, 2)).astype(BF16)
        bbi_d = _block_diag(jnp.swapaxes(bbi[:, :, :S5_CH], 1, 2)).astype(BF16)
        ccr_d = _block_diag(jnp.swapaxes(s5_c_re[l], 1, 2)).astype(BF16)
        cci_d = _block_diag(jnp.swapaxes(s5_c_im[l], 1, 2)).astype(BF16)

        u4, kt, vt, q2, k2, vb, zg, xbc, gates, dtr = _in_proj_prompt(hp, g_mix, w_all, sb_beta_bias, l, bsz)
        r3 = lambda t: t.reshape(bsz, sl, t.shape[-1])
        y5, s5r, s5i = _s5_prompt(u4.reshape(bsz, sl * 4, 128), wcat, poutr, pouti, a16r, a16i,
                                  s5d, wglu, bglu, l)
        ysb = _sb_prompt(r3(q2), r3(k2), r3(vb), tri)
        ym, ssm, cvs = _ssd_prompt(r3(xbc), r3(dtr), r3(zg), m2_conv_w, cb, dtb, a_neg, m2d, m2n, l)
        hp = _merge(hp, y5.reshape(bsz * sl * 4, 128), ysb.reshape(bsz * sl, SB_WIDTH),
                    ym.reshape(bsz * sl, M2_INNER), gates, w5, wsb, wm, wo, l)
        mk, mv, mkb, mvb = _mem_kv(mem2, wk, wv, l)
        hp = _xattn_prompt(hp.reshape(bsz, sl, D_MODEL), g_mem, wq, mkb.reshape(bsz, mt, D_MODEL),
                           mvb.reshape(bsz, mt, D_MODEL), wmo, l).reshape(bsz * sl, D_MODEL)
        hp = _mlp(hp, g_mlp, w1, w2, l)
        for lst, val in zip(outs[:8], (
                kt, vt, s5r.reshape(bsz, S5_GROUPS, S5_STATE), s5i.reshape(bsz, S5_GROUPS, S5_STATE),
                ssm.reshape(bsz, M2_HEADS, M2_HEAD_DIM, M2_STATE), cvs,
                mk.reshape(bsz, mt, MEM_HEADS, MEM_HEAD_DIM), mv.reshape(bsz, mt, MEM_HEADS, MEM_HEAD_DIM))):
            lst.append(val)

        u, k, v, qs, zg, xbc, gates, dtr = _in_proj_sample(hs, g_mix, w_all, l)
        y5, s5r, s5i = _s5_step(u, h0r, h0i, a1r, a1i, bbr_d, bbi_d, ccr_d, cci_d, s5d, wglu, bglu, l)
        ysb = _sb_decode(qs, bias_col, cache_sb_k, cache_sb_v, page_table, l)
        ym, ssm, cvs = _ssd_step(xbc, state_conv, dtr, zg, state_ssm, m2_conv_w, cb, dtb, a_neg, m2d, m2n, l)
        hs = _merge(hs, y5, ysb, ym, gates, w5, wsb, wm, wo, l)
        hs = _xattn_sample(hs, g_mem, wq, cache_mem_k, cache_mem_v, wmo, l)
        hs = _mlp(hs, g_mlp, w1, w2, l)
        for lst, val in zip(outs[8:], (
                k.reshape(dec_b, 1, SB_HEADS, SB_HEAD_DIM), v.reshape(dec_b, 1, SB_HEADS, SB_HEAD_DIM),
                s5r.reshape(dec_b, S5_GROUPS, S5_STATE), s5i.reshape(dec_b, S5_GROUPS, S5_STATE),
                ssm.reshape(dec_b, M2_HEADS, M2_HEAD_DIM, M2_STATE), cvs)):
            lst.append(val)

    g_fin = norm_final[None, :]
    y_prompt = _final_norm(hp, g_fin).reshape(bsz, sl, D_MODEL)
    y_sample = _final_norm(hs, g_fin).reshape(dec_b, 1, D_MODEL)
    stacked = [jnp.stack(o) for o in outs]
    for i in (0, 1):
        stacked[i] = jnp.transpose(stacked[i].reshape(depth, bsz, SB_HEADS, SB_HEAD_DIM, sl), (0, 1, 4, 2, 3))
    return (y_prompt, y_sample) + tuple(stacked)
```

```python
import functools
import math

import jax
import jax.numpy as jnp
from jax import lax
from jax.experimental import pallas as pl
from jax.experimental.pallas import tpu as pltpu

F32 = jnp.float32
BF16 = jnp.bfloat16

D_MODEL = 1024
PAGE_SIZE = 128
S5_CH = 16
S5_WIDTH = 512
S5_GROUPS = 32
S5_STATE = 64
S5_LBLK = 2048
S5_T = 16
SB_WIDTH = 512
SB_HEADS = 8
SB_HEAD_DIM = 64
SB_TILE = 256
M2_INNER = 512
M2_HEADS = 8
M2_HEAD_DIM = 64
M2_GROUPS = 2
M2_STATE = 128
M2_CONV = 4
M2_CONV_DIM = 1024
M2_Q = 256
MEM_HEADS = 4
MEM_HEAD_DIM = 256
D_FF = 4096
EPS = 1e-6
LOG2E = 1.4426950408889634
IN_MAIN = 3584
IN_DT = 3584
IN_GATE = 3592
VMEM_LIMIT_V7X = 56 * 1024 * 1024


def _cp(*sem):
    return pltpu.CompilerParams(dimension_semantics=sem, vmem_limit_bytes=VMEM_LIMIT_V7X)


def _tile(m, pref):
    t = min(m, pref)
    while m % t:
        t -= 8
    return t


def _full(shape):
    nd = len(shape)
    return pl.BlockSpec(shape, lambda *_: (0,) * nd)


def _layer(shape, l):
    nd = len(shape)
    return pl.BlockSpec((None,) + tuple(shape), lambda *_: (l,) + (0,) * nd)


def _rms(x, g):
    return x * lax.rsqrt(jnp.mean(x * x, axis=-1, keepdims=True) + EPS) * g


def _bdot(a, b):
    return jnp.dot(a.astype(BF16), b.astype(BF16), preferred_element_type=F32)


def _dot_nt(a, b):
    return lax.dot_general(a, b, (((1,), (1,)), ((), ())), preferred_element_type=F32)


def _dot_tn(a, b):
    return lax.dot_general(a, b, (((0,), (0,)), ((), ())), preferred_element_type=F32)


def _softplus(z):
    return jnp.maximum(z, 0.0) + jnp.log(1.0 + jnp.exp(-jnp.abs(z)))


def _sigmoid(z):
    return 1.0 / (1.0 + jnp.exp(-z))


def _silu(z):
    return z * _sigmoid(z)


def _gelu_tanh(x):
    return 0.5 * x * (1.0 + jnp.tanh(math.sqrt(2.0 / math.pi) * (x + 0.044715 * (x * x * x))))


def _in_proj_common(x_ref, g_ref, w_ref, zg_ref, xbc_ref, gate_ref, dt_ref):
    xn = _rms(x_ref[...], g_ref[...]).astype(BF16)

    def mm(lo, hi):
        return jnp.dot(xn, w_ref[:, lo:hi], preferred_element_type=F32)

    zg_ref[...] = mm(2048, 2560)
    xbc_ref[...] = mm(2560, 3584)
    gate_ref[...] = mm(3584, 6656)
    dt_ref[...] = mm(6656, 7168)
    return mm


def _in_proj_prompt_kernel(l, bias_ref, x_ref, g_ref, w_ref, u4_ref, kt_ref, vt_ref, vtb_ref, q2_ref, k2_ref,
                           zg_ref, xbc_ref, gate_ref, dt_ref):
    mm = _in_proj_common(x_ref, g_ref, w_ref, zg_ref, xbc_ref, gate_ref, dt_ref)
    tm = x_ref.shape[0]
    u = mm(0, 512)
    for j in range(4):
        u4_ref[pl.ds(j, tm, stride=4), :] = u[:, j * 128:(j + 1) * 128]
    q = mm(512, 1024) * (SB_HEAD_DIM ** -0.5)
    k = mm(1024, 1536)
    v = mm(1536, 2048)
    kt_ref[...] = k.T
    vt_ref[...] = v.T
    vtb_ref[...] = v.astype(BF16).T
    lane = lax.broadcasted_iota(jnp.int32, (tm, 128), 1)
    for h in range(SB_HEADS):
        sl = slice((h // 2) * 128, (h // 2 + 1) * 128)
        qh, kh = q[:, sl], k[:, sl]
        if h % 2:
            qh, kh = pltpu.roll(qh, 64, 1), pltpu.roll(kh, 64, 1)
        extra = lane == SB_HEAD_DIM
        q2_ref[:, h * 128:(h + 1) * 128] = jnp.where(
            lane < SB_HEAD_DIM, qh, jnp.where(extra, bias_ref[l, h], 0.0)).astype(BF16)
        k2_ref[:, h * 128:(h + 1) * 128] = jnp.where(
            lane < SB_HEAD_DIM, kh, jnp.where(extra, 1.0, 0.0)).astype(BF16)


def _in_proj_prompt(x, g, w_all, bias, l, bsz):
    m = x.shape[0]
    sl = m // bsz
    tm = _tile(sl, 256)
    per_b = sl // tm
    row = lambda n: pl.BlockSpec((tm, n), lambda i: (i, 0))
    tr = pl.BlockSpec((None, 512, tm), lambda i: (i // per_b, 0, i % per_b))
    outs = [(1024, BF16), (1024, BF16), (512, F32), (1024, F32), (3072, F32), (512, F32)]
    return pl.pallas_call(
        functools.partial(_in_proj_prompt_kernel, l),
        grid=(m // tm,),
        in_specs=[pl.BlockSpec(memory_space=pltpu.SMEM),
                  row(D_MODEL), _layer((1, D_MODEL), l), _layer((D_MODEL, 7168), l)],
        out_specs=[pl.BlockSpec((tm * 4, 128), lambda i: (i, 0)), tr, tr, tr] + [row(n) for n, _ in outs],
        out_shape=[jax.ShapeDtypeStruct((m * 4, 128), F32),
                   jax.ShapeDtypeStruct((bsz, 512, sl), F32), jax.ShapeDtypeStruct((bsz, 512, sl), F32),
                   jax.ShapeDtypeStruct((bsz, 512, sl), BF16)]
                  + [jax.ShapeDtypeStruct((m, n), dt) for n, dt in outs],
        compiler_params=_cp("parallel"),
        name="in_proj",
    )(bias, x, g, w_all)


def _in_proj_sample_kernel(x_ref, g_ref, w_ref, u_ref, k_ref, v_ref, q_ref, zg_ref, xbc_ref, gate_ref, dt_ref):
    mm = _in_proj_common(x_ref, g_ref, w_ref, zg_ref, xbc_ref, gate_ref, dt_ref)
    u_ref[...] = mm(0, 512)
    q_ref[...] = mm(512, 1024) * (SB_HEAD_DIM ** -0.5)
    k_ref[...] = mm(1024, 1536)
    v_ref[...] = mm(1536, 2048)


def _in_proj_sample(x, g, w_all, l):
    m = x.shape[0]
    outs = [512, 512, 512, 512, 512, 1024, 3072, 512]
    return pl.pallas_call(
        _in_proj_sample_kernel,
        grid=(1,),
        in_specs=[_full((m, D_MODEL)), _layer((1, D_MODEL), l), _layer((D_MODEL, 7168), l)],
        out_specs=[_full((m, n)) for n in outs],
        out_shape=[jax.ShapeDtypeStruct((m, n), F32) for n in outs],
        compiler_params=_cp("arbitrary"),
        name="in_proj_step",
    )(x, g, w_all)


def _s5_prep_kernel(lr_row, li_row, lr_col, li_col, ldt_ref, btr_ref, bti_ref, ctr_ref, cti_ref,
                    wcat_ref, poutr_ref, pouti_ref, pw_ref, bbr_ref, bbi_ref):
    dt = jnp.exp(ldt_ref[0])
    lr, li = lr_row[0], li_row[0]
    kk = lax.broadcasted_iota(jnp.int32, (24, 1), 0).astype(F32)
    mag = jnp.exp(kk * (lr * dt))
    ang = kk * (li * dt)
    p_re, p_im = mag * jnp.cos(ang), mag * jnp.sin(ang)
    pw_ref[0, 0] = p_re
    pw_ref[0, 1] = p_im

    def rep(p, lo):
        return jnp.broadcast_to(p[lo:lo + S5_T][:, None, :], (S5_T, S5_CH, S5_STATE)).reshape(256, S5_STATE)

    c_re, c_im = ctr_ref[0], cti_ref[0]
    pr0, pi0 = rep(p_re, 0), rep(p_im, 0)
    l_re = c_re * pr0 - c_im * pi0
    l_im = c_re * pi0 + c_im * pr0
    pr1, pi1 = rep(p_re, 1), rep(p_im, 1)
    poutr_ref[0] = (c_re * pr1 - c_im * pi1).astype(BF16)
    pouti_ref[0] = (-(c_re * pi1 + c_im * pr1)).astype(BF16)

    lrc, lic = lr_col[0], li_col[0]
    m1 = jnp.exp(lrc * dt)
    a_re, a_im = m1 * jnp.cos(lic * dt), m1 * jnp.sin(lic * dt)
    den = lrc * lrc + lic * lic
    f_re = ((a_re - 1.0) * lrc + a_im * lic) / den
    f_im = (a_im * lrc - (a_re - 1.0) * lic) / den
    b_re, b_im = btr_ref[0], bti_ref[0]
    bb_re = f_re * b_re - f_im * b_im
    bb_im = f_re * b_im + f_im * b_re
    bbr_ref[0] = bb_re
    bbi_ref[0] = bb_im

    hp = lax.Precision.HIGHEST
    kt = (jnp.dot(l_re, bb_re, precision=hp, preferred_element_type=F32)
          - jnp.dot(l_im, bb_im, precision=hp, preferred_element_type=F32))
    lane_s = lax.broadcasted_iota(jnp.int32, (1, 256), 1) // S5_CH
    toep = jnp.where(lane_s == 0, kt, 0.0)
    for s in range(1, S5_T):
        sh = jnp.concatenate([jnp.zeros((S5_CH * s, 256), F32), kt[:256 - S5_CH * s]], axis=0)
        toep = jnp.where(lane_s == s, sh, toep)

    ks = (S5_T - 1 - lane_s).astype(F32)
    magc = jnp.exp(ks * (lrc * dt))
    angc = ks * (lic * dt)
    q_re, q_im = magc * jnp.cos(angc), magc * jnp.sin(angc)
    wcat_ref[0, 0:256] = toep.astype(BF16)
    wcat_ref[0, 256:320] = (q_re * bb_re - q_im * bb_im).astype(BF16)
    wcat_ref[0, 320:384] = (q_re * bb_im + q_im * bb_re).astype(BF16)


def _s5_prep(lam_re, lam_im, log_dt, b_re, b_im, c_re, c_im):
    g = lam_re.shape[0]
    per = lambda *shape: pl.BlockSpec((1,) + shape, lambda i: (i,) + (0,) * len(shape))
    tile_b = lambda b: jnp.tile(b, (1, 1, S5_T))
    tile_c = lambda c: jnp.tile(c, (1, S5_T, 1))
    return pl.pallas_call(
        _s5_prep_kernel,
        grid=(g,),
        in_specs=[per(1, 64), per(1, 64), per(64, 1), per(64, 1), per(1, 1),
                  per(64, 256), per(64, 256), per(256, 64), per(256, 64)],
        out_specs=[per(384, 256), per(256, 64), per(256, 64), per(2, 24, 64), per(64, 256), per(64, 256)],
        out_shape=[jax.ShapeDtypeStruct((g, 384, 256), BF16),
                   jax.ShapeDtypeStruct((g, 256, 64), BF16),
                   jax.ShapeDtypeStruct((g, 256, 64), BF16),
                   jax.ShapeDtypeStruct((g, 2, 24, 64), F32),
                   jax.ShapeDtypeStruct((g, 64, 256), F32),
                   jax.ShapeDtypeStruct((g, 64, 256), F32)],
        compiler_params=_cp("parallel"),
        name="s5_prep",
    )(lam_re[:, None, :], lam_im[:, None, :], lam_re[:, :, None], lam_im[:, :, None],
      log_dt[:, None, None], tile_b(b_re), tile_b(b_im), tile_c(c_re), tile_c(c_im))


def _rows(u_ref, s, j, n):
    return u_ref[0, pl.ds(s * 4 + j, n, stride=4 * S5_T), :]


def _s5_prompt_kernel(u_ref, wcat_ref, poutr_ref, pouti_ref, ar_ref, ai_ref, d_ref, wglu_ref, bglu_ref,
                      y_ref, sr_ref, si_ref, v_s, yt_s, hr_s, hi_s, cr_s, ci_s):
    n = v_s.shape[2]
    lb = pl.program_id(1)

    @pl.when(lb == 0)
    def _():
        cr_s[...] = jnp.zeros_like(cr_s)
        ci_s[...] = jnp.zeros_like(ci_s)

    for s in range(S5_T):
        for j in range(4):
            v_s[s, j * 128:(j + 1) * 128, :] = _rows(u_ref, s, j, n).T.astype(BF16)

    for gp in range(S5_GROUPS // 2):
        res = []
        for g in (2 * gp, 2 * gp + 1):
            vg = v_s[:, g * S5_CH:(g + 1) * S5_CH, :].reshape(S5_T * S5_CH, n)
            r = jnp.dot(wcat_ref[g], vg, preferred_element_type=F32)
            yt_s[g] = r[0:256]
            res.append(r)
        hr_s[:, gp * 128:(gp + 1) * 128] = jnp.concatenate([res[0][256:320], res[1][256:320]], axis=0).T
        hi_s[:, gp * 128:(gp + 1) * 128] = jnp.concatenate([res[0][320:384], res[1][320:384]], axis=0).T

    a_re, a_im = ar_ref[...], ai_ref[...]

    def step(i, carry):
        h_re, h_im = carry
        s_re = hr_s[pl.ds(i, 1), :]
        s_im = hi_s[pl.ds(i, 1), :]
        hr_s[pl.ds(i, 1), :] = h_re
        hi_s[pl.ds(i, 1), :] = h_im
        return (a_re * h_re - a_im * h_im + s_re, a_re * h_im + a_im * h_re + s_im)

    h_re, h_im = lax.fori_loop(0, n, step, (cr_s[...], ci_s[...]))
    cr_s[...] = h_re
    ci_s[...] = h_im
    sr_ref[0] = h_re
    si_ref[0] = h_im

    for gp in range(S5_GROUPS // 2):
        ht_re = hr_s[:, gp * 128:(gp + 1) * 128].T.astype(BF16)
        ht_im = hi_s[:, gp * 128:(gp + 1) * 128].T.astype(BF16)
        for j, g in enumerate((2 * gp, 2 * gp + 1)):
            yt_s[g] += (jnp.dot(poutr_ref[g], ht_re[64 * j:64 * j + 64], preferred_element_type=F32)
                        + jnp.dot(pouti_ref[g], ht_im[64 * j:64 * j + 64], preferred_element_type=F32))

    for t in range(S5_T):
        yt = yt_s[:, t * S5_CH:(t + 1) * S5_CH, :].reshape(S5_WIDTH, n).T
        y = yt + d_ref[...] * jnp.concatenate([_rows(u_ref, t, j, n) for j in range(4)], axis=1)
        gl = _gelu_tanh(y)
        o = gl * _sigmoid(jnp.dot(gl.astype(BF16), wglu_ref[...], preferred_element_type=F32) + bglu_ref[...])
        for j in range(4):
            y_ref[0, pl.ds(t * 4 + j, n, stride=4 * S5_T), :] = o[:, j * 128:(j + 1) * 128]


def _s5_prompt(u4, wcat, poutr, pouti, a16r, a16i, d, wglu, bglu, l):
    b, sl = u4.shape[0], u4.shape[1] // 4
    lblk = _tile(sl, S5_LBLK)
    n = lblk // S5_T
    grp = lambda r, c: pl.BlockSpec((S5_GROUPS, r, c), lambda i, j: (l, 0, 0))
    return pl.pallas_call(
        _s5_prompt_kernel,
        grid=(b, sl // lblk),
        in_specs=[pl.BlockSpec((1, lblk * 4, 128), lambda i, j: (i, j, 0)),
                  grp(384, 256), grp(256, 64), grp(256, 64),
                  _layer((1, 2048), l), _layer((1, 2048), l),
                  _layer((1, S5_WIDTH), l), _layer((S5_WIDTH, S5_WIDTH), l), _layer((1, S5_WIDTH), l)],
        out_specs=[pl.BlockSpec((1, lblk * 4, 128), lambda i, j: (i, j, 0)),
                   pl.BlockSpec((1, 1, 2048), lambda i, j: (i, 0, 0)),
                   pl.BlockSpec((1, 1, 2048), lambda i, j: (i, 0, 0))],
        out_shape=[jax.ShapeDtypeStruct((b, sl * 4, 128), F32),
                   jax.ShapeDtypeStruct((b, 1, 2048), F32),
                   jax.ShapeDtypeStruct((b, 1, 2048), F32)],
        scratch_shapes=[pltpu.VMEM((S5_T, S5_WIDTH, n), BF16),
                        pltpu.VMEM((S5_GROUPS, 256, n), F32),
                        pltpu.VMEM((n, 2048), F32), pltpu.VMEM((n, 2048), F32),
                        pltpu.VMEM((1, 2048), F32), pltpu.VMEM((1, 2048), F32)],
        compiler_params=_cp("parallel", "arbitrary"),
        name="s5_prompt",
    )(u4, wcat, poutr, pouti, a16r, a16i, d, wglu, bglu)


def _s5_step_kernel(u_ref, h0r_ref, h0i_ref, ar_ref, ai_ref, bbr_ref, bbi_ref, ccr_ref, cci_ref,
                    d_ref, wglu_ref, bglu_ref, y_ref, hr_ref, hi_ref):
    u = u_ref[...]
    ub = u.astype(BF16)
    a_re, a_im = ar_ref[...], ai_ref[...]
    h0r, h0i = h0r_ref[...], h0i_ref[...]
    h_re = a_re * h0r - a_im * h0i + jnp.dot(ub, bbr_ref[...], preferred_element_type=F32)
    h_im = a_re * h0i + a_im * h0r + jnp.dot(ub, bbi_ref[...], preferred_element_type=F32)
    hr_ref[...] = h_re
    hi_ref[...] = h_im
    y = (jnp.dot(h_re.astype(BF16), ccr_ref[...], preferred_element_type=F32)
         - jnp.dot(h_im.astype(BF16), cci_ref[...], preferred_element_type=F32) + d_ref[...] * u)
    gl = _gelu_tanh(y)
    y_ref[...] = gl * _sigmoid(jnp.dot(gl.astype(BF16), wglu_ref[...], preferred_element_type=F32) + bglu_ref[...])


def _s5_step(u, h0r, h0i, a1r, a1i, bbr, bbi, ccr, cci, d, wglu, bglu, l):
    m = u.shape[0]
    return pl.pallas_call(
        _s5_step_kernel,
        grid=(1,),
        in_specs=[_full((m, S5_WIDTH)), _layer((m, 2048), l), _layer((m, 2048), l),
                  _layer((1, 2048), l), _layer((1, 2048), l),
                  _layer((S5_WIDTH, 2048), l), _layer((S5_WIDTH, 2048), l),
                  _layer((2048, S5_WIDTH), l), _layer((2048, S5_WIDTH), l),
                  _layer((1, S5_WIDTH), l), _layer((S5_WIDTH, S5_WIDTH), l), _layer((1, S5_WIDTH), l)],
        out_specs=[_full((m, S5_WIDTH)), _full((m, 2048)), _full((m, 2048))],
        out_shape=[jax.ShapeDtypeStruct((m, S5_WIDTH), F32),
                   jax.ShapeDtypeStruct((m, 2048), F32), jax.ShapeDtypeStruct((m, 2048), F32)],
        compiler_params=_cp("arbitrary"),
        name="s5_step",
    )(u, h0r, h0i, a1r, a1i, bbr, bbi, ccr, cci, d, wglu, bglu)


def _sb_prompt_kernel(q_ref, k_ref, vt_ref, tri_ref, o_ref, acc_s, c_s):
    tq, tk, tc = q_ref.shape[1], SB_TILE, 256
    per = tq // tk
    qi = pl.program_id(2)
    chunks = [(hh, a) for hh in range(2) for a in range(tq // tc)]
    qs = [q_ref[0, a * tc:(a + 1) * tc, hh * 128:(hh + 1) * 128] for hh, a in chunks]
    key = lax.broadcasted_iota(jnp.int32, (tk, tc), 0)
    qry = lax.broadcasted_iota(jnp.int32, (tk, tc), 1)
    acc_s[...] = jnp.zeros_like(acc_s)
    c_s[...] = jnp.zeros_like(c_s)

    def block(kb, off):
        start = pl.multiple_of(kb * tk, tk)
        kblk = k_ref[0, pl.ds(start, tk), :]
        vt = vt_ref[0, :, pl.ds(start, tk)]
        live, masks = [], {}
        for i, (hh, a) in enumerate(chunks):
            if off is not None and off >= (a + 1) * tc - 1:
                continue
            live.append(i)
            if off is not None and off + tk - 1 >= a * tc:
                masks[i] = key + (off - a * tc) < qry
        def logits(i):
            hh = chunks[i][0]
            return _dot_nt(kblk[:, hh * 128:(hh + 1) * 128], qs[i])

        zs, rs = {live[0]: logits(live[0])}, {}
        for n_done, i in enumerate(live):
            if n_done + 1 < len(live):
                zs[live[n_done + 1]] = logits(live[n_done + 1])
            z = zs[i]
            sp = jnp.maximum(z, 0.0) + jnp.log(1.0 + jnp.exp2(jnp.abs(z) * (-LOG2E)))
            if i in masks:
                sp = jnp.where(masks[i], sp, 0.0)
            rs[i] = jnp.dot(tri_ref[...], sp.astype(BF16), preferred_element_type=F32)
        for i in live:
            c = c_s[i]
            w = jnp.exp(zs[i] - rs[i] - c)
            if i in masks:
                w = jnp.where(masks[i], w, 0.0)
            acc_s[i] += jnp.dot(vt, w.astype(BF16), preferred_element_type=F32)
            c_s[i] = c + rs[i][0:1, :]

    for d in range(per - 1, -1, -1):
        block(qi * per + d, d * tk)
    n = qi * per

    @pl.loop(0, n // per)
    def _(j):
        for d in range(per):
            block(n - 1 - j * per - d, None)

    chan = lax.broadcasted_iota(jnp.int32, (128, tc), 0)
    for a in range(tq // tc):
        o_ref[0, a * tc:(a + 1) * tc, :] = jnp.where(
            chan < SB_HEAD_DIM, acc_s[a], acc_s[tq // tc + a]).T


def _sb_prompt(q2, k2, vtb, tri_t):
    b, _, sl = vtb.shape
    tq = _tile(sl, 2 * SB_TILE)
    return pl.pallas_call(
        _sb_prompt_kernel,
        grid=(b, SB_HEADS // 2, sl // tq),
        in_specs=[pl.BlockSpec((1, tq, 256), lambda i, p, j: (i, j, p)),
                  pl.BlockSpec((1, sl, 256), lambda i, p, j: (i, 0, p)),
                  pl.BlockSpec((1, 128, sl), lambda i, p, j: (i, p, 0)),
                  _full((SB_TILE, SB_TILE))],
        out_specs=pl.BlockSpec((1, tq, 128), lambda i, p, j: (i, j, p)),
        out_shape=jax.ShapeDtypeStruct((b, sl, SB_WIDTH), F32),
        scratch_shapes=[pltpu.VMEM((2 * tq // 256, 128, 256), F32), pltpu.VMEM((2 * tq // 256, 1, 256), F32)],
        compiler_params=_cp("parallel", "parallel", "arbitrary"),
        name="sb_prompt",
    )(q2, k2, vtb, tri_t)


def _split_bf16(x):
    hi = x.astype(BF16)
    return hi, (x - hi.astype(F32)).astype(BF16)


def _sb_decode_kernel(npg, pt_ref, q_ref, bias_ref, tri_ref, later_ref, *refs):
    k_refs, v_refs = refs[:npg], refs[npg:2 * npg]
    o_ref, qc_s, acc_s, c_s = refs[2 * npg:]
    j = pl.program_id(1)

    @pl.when(j == 0)
    def _():
        qc_s[...] = jnp.broadcast_to(q_ref[0], (128, SB_WIDTH)).T
        acc_s[...] = jnp.zeros_like(acc_s)
        c_s[...] = jnp.zeros_like(c_s)

    qc = qc_s[...]
    bias = jnp.concatenate([bias_ref[...]] * npg, axis=0)
    z = jnp.concatenate(
        [(k_refs[i][...] * qc).reshape(SB_HEADS, SB_HEAD_DIM, PAGE_SIZE).sum(axis=1) for i in range(npg)],
        axis=0) + bias
    lk = -_softplus(z)
    hi, lo = _split_bf16(lk)
    r_loc = jnp.dot(jnp.concatenate([hi, lo], axis=1), tri_ref[...], preferred_element_type=F32)
    tot = jnp.broadcast_to(r_loc[:, 0:1], r_loc.shape)
    hi, lo = _split_bf16(tot)
    c_in = c_s[...]
    r = (r_loc + jnp.dot(later_ref[...], jnp.concatenate([hi, lo], axis=0), preferred_element_type=F32)
         + jnp.concatenate([c_in] * npg, axis=0))
    w = jnp.exp(z + r)
    c_s[...] = r[0:SB_HEADS] - r_loc[0:SB_HEADS] + tot[0:SB_HEADS]
    acc = acc_s[...]
    for i in range(npg):
        w_rows = jnp.broadcast_to(w[i * SB_HEADS:(i + 1) * SB_HEADS][:, None, :],
                                  (SB_HEADS, SB_HEAD_DIM, PAGE_SIZE)).reshape(SB_WIDTH, PAGE_SIZE)
        acc = acc + v_refs[i][...] * w_rows
    acc_s[...] = acc

    @pl.when(j == pl.num_programs(1) - 1)
    def _():
        o_ref[0] = jnp.sum(acc.T, axis=0, keepdims=True)


def _sb_decode(q, bias_col, cache_k, cache_v, page_table, l):
    m = q.shape[0]
    n_pages = page_table.shape[1]
    npg = 16
    while n_pages % npg:
        npg //= 2
    steps = n_pages // npg
    rows = npg * SB_HEADS

    def page(i):
        return pl.BlockSpec((None, None, SB_WIDTH, PAGE_SIZE),
                            lambda b, j, pt: (l, pt[b, (steps - 1 - j) * npg + i], 0, 0))

    const = lambda shape: pl.BlockSpec(shape, lambda b, j, pt: (0,) * len(shape))
    grid_spec = pltpu.PrefetchScalarGridSpec(
        num_scalar_prefetch=1,
        grid=(m, steps),
        in_specs=[pl.BlockSpec((1, 1, SB_WIDTH), lambda b, j, pt: (b, 0, 0)),
                  pl.BlockSpec((None, SB_HEADS, 128), lambda b, j, pt: (l, 0, 0)),
                  const((2 * PAGE_SIZE, PAGE_SIZE)), const((rows, 2 * rows))]
                 + [page(i) for i in range(npg)] * 2,
        out_specs=pl.BlockSpec((1, 1, SB_WIDTH), lambda b, j, pt: (b, 0, 0)),
        scratch_shapes=[pltpu.VMEM((SB_WIDTH, 128), F32), pltpu.VMEM((SB_WIDTH, PAGE_SIZE), F32),
                        pltpu.VMEM((SB_HEADS, 128), F32)],
    )
    depth, n_pool = cache_k.shape[:2]
    ck = jnp.transpose(cache_k, (0, 1, 3, 4, 2)).reshape(depth, n_pool, SB_WIDTH, PAGE_SIZE)
    cv = jnp.transpose(cache_v, (0, 1, 3, 4, 2)).reshape(depth, n_pool, SB_WIDTH, PAGE_SIZE)
    tri = (jnp.arange(2 * PAGE_SIZE)[:, None] % PAGE_SIZE >= jnp.arange(PAGE_SIZE)[None, :]).astype(BF16)
    ridx = jnp.arange(rows)
    later = ((ridx[None, :] % SB_HEADS == ridx[:, None] % SB_HEADS)
             & (ridx[None, :] // SB_HEADS > ridx[:, None] // SB_HEADS))
    later = jnp.concatenate([later, later], axis=1).astype(BF16)
    out = pl.pallas_call(
        functools.partial(_sb_decode_kernel, npg),
        grid_spec=grid_spec,
        out_shape=jax.ShapeDtypeStruct((m, 1, SB_WIDTH), F32),
        compiler_params=_cp("parallel", "arbitrary"),
        name="sb_decode",
    )(page_table, q.reshape(m, 1, SB_WIDTH), bias_col, tri, later, *([ck] * npg), *([cv] * npg))
    return out.reshape(m, SB_WIDTH)


def _cumsum_rows(x):
    n = x.shape[0]
    row = lax.broadcasted_iota(jnp.int32, (n, 1), 0)
    k = 1
    while k < n:
        x = x + jnp.where(row >= k, pltpu.roll(x, k, 0), 0.0)
        k *= 2
    return x


def _ssd_prompt_kernel(xbc_ref, dt_ref, zg_ref, cw_ref, cb_ref, dtb_ref, a_ref, d_ref, nrm_ref,
                       y_ref, st_ref, cv_ref, prev_s, st_s):
    q = xbc_ref.shape[1]
    c = pl.program_id(1)

    @pl.when(c == 0)
    def _():
        prev_s[...] = jnp.zeros_like(prev_s)
        st_s[...] = jnp.zeros_like(st_s)

    u = xbc_ref[0]
    prev = prev_s[...]
    row8 = lax.broadcasted_iota(jnp.int32, (8, 1), 0)
    conv = cb_ref[...] + cw_ref[M2_CONV - 1:M2_CONV, :] * u
    for k in range(1, M2_CONV):
        ru = pltpu.roll(u, k, 0)
        top = jnp.where(row8 < k, pltpu.roll(prev, k, 0), ru[0:8])
        conv = conv + cw_ref[M2_CONV - 1 - k:M2_CONV - k, :] * jnp.concatenate([top, ru[8:]], axis=0)
    prev_s[...] = u[q - 8:q]
    xc = _silu(conv)
    xs = xc[:, 0:M2_INNER]
    dt = _softplus(dt_ref[0] + dtb_ref[...])
    acum = _cumsum_rows(dt * a_ref[...])
    a_last = acum[q - 1:q, :]
    xdt = xs * dt
    xw = (xdt * jnp.exp(a_last - acum)).astype(BF16)
    xdt_b = xdt.astype(BF16)
    e_acum = jnp.exp(acum)
    rows = lax.broadcasted_iota(jnp.int32, (q, q), 0)
    cols = lax.broadcasted_iota(jnp.int32, (q, q), 1)
    causal = cols <= rows
    lane = lax.broadcasted_iota(jnp.int32, (q, 128), 1)
    y_parts = []
    for g in range(M2_GROUPS):
        bm = xc[:, M2_INNER + g * M2_STATE:M2_INNER + (g + 1) * M2_STATE].astype(BF16)
        cm = xc[:, M2_INNER + (M2_GROUPS + g) * M2_STATE:M2_INNER + (M2_GROUPS + g + 1) * M2_STATE].astype(BF16)
        cb = _dot_nt(cm, bm)
        st = st_s[g]
        y_off = jnp.dot(cm, st.astype(BF16), preferred_element_type=F32) * e_acum[:, g * 256:(g + 1) * 256]
        st_s[g] = st * jnp.exp(a_last[:, g * 256:(g + 1) * 256]) + _dot_tn(bm, xw[:, g * 256:(g + 1) * 256])
        for pp in range(2):
            p = 2 * g + pp
            a_pair = acum[:, p * 128:(p + 1) * 128]
            a_t = a_pair.T
            yd = []
            for hh in range(2):
                seg = a_pair[:, 64 * hh:64 * hh + 1] - a_t[64 * hh:64 * hh + 1, :]
                mix = jnp.where(causal, cb * jnp.exp(jnp.minimum(seg, 0.0)), 0.0).astype(BF16)
                yd.append(jnp.dot(mix, xdt_b[:, p * 128:(p + 1) * 128], preferred_element_type=F32))
            y_parts.append(jnp.where(lane < M2_HEAD_DIM, yd[0], yd[1]) + y_off[:, pp * 128:(pp + 1) * 128])
    y = jnp.concatenate(y_parts, axis=1) + d_ref[...] * xs
    y_ref[0] = _rms(y * _silu(zg_ref[0]), nrm_ref[...])

    @pl.when(c == pl.num_programs(1) - 1)
    def _():
        for g in range(M2_GROUPS):
            st_ref[0, g * 256:(g + 1) * 256, :] = st_s[g].T
        cv_ref[0] = xbc_ref[0, q - (M2_CONV - 1):q, :]


def _ssd_prompt(xbc, dt_raw, zg, cw, cb, dtb, a, d, nrm, l):
    b, sl, _ = xbc.shape
    q = _tile(sl, M2_Q)
    tok = lambda n: pl.BlockSpec((1, q, n), lambda i, j: (i, j, 0))
    return pl.pallas_call(
        _ssd_prompt_kernel,
        grid=(b, sl // q),
        in_specs=[tok(M2_CONV_DIM), tok(M2_INNER), tok(M2_INNER),
                  _layer((M2_CONV, M2_CONV_DIM), l), _layer((1, M2_CONV_DIM), l),
                  _layer((1, M2_INNER), l), _layer((1, M2_INNER), l), _layer((1, M2_INNER), l),
                  _layer((1, M2_INNER), l)],
        out_specs=[tok(M2_INNER),
                   pl.BlockSpec((1, M2_INNER, M2_STATE), lambda i, j: (i, 0, 0)),
                   pl.BlockSpec((1, M2_CONV - 1, M2_CONV_DIM), lambda i, j: (i, 0, 0))],
        out_shape=[jax.ShapeDtypeStruct((b, sl, M2_INNER), F32),
                   jax.ShapeDtypeStruct((b, M2_INNER, M2_STATE), F32),
                   jax.ShapeDtypeStruct((b, M2_CONV - 1, M2_CONV_DIM), F32)],
        scratch_shapes=[pltpu.VMEM((8, M2_CONV_DIM), F32), pltpu.VMEM((M2_GROUPS, M2_STATE, 256), F32)],
        compiler_params=_cp("parallel", "arbitrary"),
        name="ssd_prompt",
    )(xbc, dt_raw, zg, cw, cb, dtb, a, d, nrm)


def _ssd_step_kernel(xbc_ref, cs_ref, dt_ref, zg_ref, st_ref, cw_ref, cb_ref, dtb_ref, a_ref, d_ref, nrm_ref,
                     y_ref, sto_ref, cvo_ref):
    nb = xbc_ref.shape[0]
    x = xbc_ref[...]
    b0, b1, b2 = cs_ref[:, 0, :], cs_ref[:, 1, :], cs_ref[:, 2, :]
    conv = (cb_ref[...] + cw_ref[0:1, :] * b0 + cw_ref[1:2, :] * b1 + cw_ref[2:3, :] * b2 + cw_ref[3:4, :] * x)
    cvo_ref[:, 0, :] = b1
    cvo_ref[:, 1, :] = b2
    cvo_ref[:, 2, :] = x
    xc = _silu(conv)
    xs = xc[:, 0:M2_INNER]
    dt = _softplus(dt_ref[...] + dtb_ref[...])
    dec = jnp.exp(dt * a_ref[...])
    pad = jnp.zeros((128 - nb, M2_INNER), F32)
    xdt_t = jnp.concatenate([xs * dt, pad], axis=0).T
    dec_t = jnp.concatenate([dec, pad], axis=0).T
    hrow = lax.broadcasted_iota(jnp.int32, (M2_INNER, 1), 0)
    lane = lax.broadcasted_iota(jnp.int32, (1, 128), 1)
    ycols = jnp.zeros((M2_INNER, 128), F32)
    for b in range(nb):
        brow = jnp.where(hrow < 256, xc[b:b + 1, 512:640], xc[b:b + 1, 640:768])
        crow = jnp.where(hrow < 256, xc[b:b + 1, 768:896], xc[b:b + 1, 896:1024])
        h_new = dec_t[:, b:b + 1] * st_ref[b] + xdt_t[:, b:b + 1] * brow
        sto_ref[b] = h_new
        ycols = jnp.where(lane == b, jnp.sum(h_new * crow, axis=1, keepdims=True), ycols)
    y = ycols.T[0:nb] + d_ref[...] * xs
    y_ref[...] = _rms(y * _silu(zg_ref[...]), nrm_ref[...])


def _ssd_step(xbc, conv_state, dt_raw, zg, ssm_state, cw, cb, dtb, a, d, nrm, l):
    m = xbc.shape[0]
    nb = 8
    tok = lambda n: pl.BlockSpec((nb, n), lambda i: (i, 0))
    depth = ssm_state.shape[0]
    st = ssm_state.reshape(depth, m, M2_INNER, M2_STATE)
    return pl.pallas_call(
        _ssd_step_kernel,
        grid=(m // nb,),
        in_specs=[tok(M2_CONV_DIM),
                  pl.BlockSpec((None, nb, M2_CONV - 1, M2_CONV_DIM), lambda i: (l, i, 0, 0)),
                  tok(M2_INNER), tok(M2_INNER),
                  pl.BlockSpec((None, nb, M2_INNER, M2_STATE), lambda i: (l, i, 0, 0)),
                  _layer((M2_CONV, M2_CONV_DIM), l), _layer((1, M2_CONV_DIM), l),
                  _layer((1, M2_INNER), l), _layer((1, M2_INNER), l), _layer((1, M2_INNER), l),
                  _layer((1, M2_INNER), l)],
        out_specs=[tok(M2_INNER),
                   pl.BlockSpec((nb, M2_INNER, M2_STATE), lambda i: (i, 0, 0)),
                   pl.BlockSpec((nb, M2_CONV - 1, M2_CONV_DIM), lambda i: (i, 0, 0))],
        out_shape=[jax.ShapeDtypeStruct((m, M2_INNER), F32),
                   jax.ShapeDtypeStruct((m, M2_INNER, M2_STATE), F32),
                   jax.ShapeDtypeStruct((m, M2_CONV - 1, M2_CONV_DIM), F32)],
        compiler_params=_cp("parallel"),
        name="ssd_step",
    )(xbc, conv_state, dt_raw, zg, st, cw, cb, dtb, a, d, nrm)


def _merge_kernel(split5, h_ref, y5_ref, ysb_ref, ym_ref, gate_ref, w5_ref, wsb_ref, wm_ref, wo_ref, o_ref):
    tm = h_ref.shape[0]

    def branch(y, w_ref, lo):
        return _sigmoid(gate_ref[:, lo:lo + D_MODEL]) * jnp.dot(
            y.astype(BF16), w_ref[...], preferred_element_type=F32)

    if split5:
        y5 = jnp.concatenate([y5_ref[pl.ds(j, tm, stride=4), :] for j in range(4)], axis=1)
    else:
        y5 = y5_ref[...]
    merged = (branch(y5, w5_ref, 0) + branch(ysb_ref[...], wsb_ref, D_MODEL)
              + branch(ym_ref[...], wm_ref, 2 * D_MODEL))
    o_ref[...] = h_ref[...] + jnp.dot(merged.astype(BF16), wo_ref[...], preferred_element_type=F32)


def _merge(h, y5, ysb, ym, gates, w5, wsb, wm, wo, l):
    m = h.shape[0]
    tm = _tile(m, 512)
    row = lambda n: pl.BlockSpec((tm, n), lambda i: (i, 0))
    split5 = y5.shape[-1] == 128
    y5_spec = pl.BlockSpec((tm * 4, 128), lambda i: (i, 0)) if split5 else row(512)
    return pl.pallas_call(
        functools.partial(_merge_kernel, split5),
        grid=(m // tm,),
        in_specs=[row(D_MODEL), y5_spec, row(512), row(512), row(3 * D_MODEL),
                  _layer((512, D_MODEL), l), _layer((512, D_MODEL), l), _layer((512, D_MODEL), l),
                  _layer((D_MODEL, D_MODEL), l)],
        out_specs=row(D_MODEL),
        out_shape=jax.ShapeDtypeStruct((m, D_MODEL), F32),
        compiler_params=_cp("parallel"),
        name="merge",
    )(h, y5, ysb, ym, gates, w5, wsb, wm, wo)


def _mem_kv_kernel(x_ref, wk_ref, wv_ref, k_ref, v_ref, kb_ref, vb_ref):
    xb = x_ref[...].astype(BF16)
    k = jnp.dot(xb, wk_ref[...], preferred_element_type=F32)
    v = jnp.dot(xb, wv_ref[...], preferred_element_type=F32)
    k_ref[...] = k
    v_ref[...] = v
    kb_ref[...] = k.astype(BF16)
    vb_ref[...] = v.astype(BF16)


def _mem_kv(mem, wk, wv, l):
    m = mem.shape[0]
    tm = _tile(m, 512)
    row = pl.BlockSpec((tm, D_MODEL), lambda i: (i, 0))
    return pl.pallas_call(
        _mem_kv_kernel,
        grid=(m // tm,),
        in_specs=[row, _layer((D_MODEL, D_MODEL), l), _layer((D_MODEL, D_MODEL), l)],
        out_specs=[row] * 4,
        out_shape=[jax.ShapeDtypeStruct((m, D_MODEL), F32)] * 2 + [jax.ShapeDtypeStruct((m, D_MODEL), BF16)] * 2,
        compiler_params=_cp("parallel"),
        name="mem_kv",
    )(mem, wk, wv)


def _xattn_prompt_kernel(h_ref, g_ref, wq_ref, mk_ref, mv_ref, wo_ref, o_ref):
    h = h_ref[0]
    xn = _rms(h, g_ref[...]).astype(BF16)
    q = (jnp.dot(xn, wq_ref[...], preferred_element_type=F32) * (MEM_HEAD_DIM ** -0.5)).astype(BF16)
    heads = []
    for hd in range(MEM_HEADS):
        sl = slice(hd * MEM_HEAD_DIM, (hd + 1) * MEM_HEAD_DIM)
        s = _dot_nt(q[:, sl], mk_ref[0, :, sl])
        e = jnp.exp(s - jnp.max(s, axis=-1, keepdims=True))
        p = e / jnp.sum(e, axis=-1, keepdims=True)
        heads.append(jnp.dot(p.astype(BF16), mv_ref[0, :, sl], preferred_element_type=F32))
    o = jnp.concatenate(heads, axis=1).astype(BF16)
    o_ref[0] = h + jnp.dot(o, wo_ref[...], preferred_element_type=F32)


def _xattn_prompt(h, g, wq, mkb, mvb, wo, l):
    b, sl, _ = h.shape
    tq = _tile(sl, 512)
    mt = mkb.shape[1]
    return pl.pallas_call(
        _xattn_prompt_kernel,
        grid=(b, sl // tq),
        in_specs=[pl.BlockSpec((1, tq, D_MODEL), lambda i, j: (i, j, 0)),
                  _layer((1, D_MODEL), l), _layer((D_MODEL, D_MODEL), l),
                  pl.BlockSpec((1, mt, D_MODEL), lambda i, j: (i, 0, 0)),
                  pl.BlockSpec((1, mt, D_MODEL), lambda i, j: (i, 0, 0)),
                  _layer((D_MODEL, D_MODEL), l)],
        out_specs=pl.BlockSpec((1, tq, D_MODEL), lambda i, j: (i, j, 0)),
        out_shape=jax.ShapeDtypeStruct((b, sl, D_MODEL), F32),
        compiler_params=_cp("parallel", "parallel"),
        name="xattn_prompt",
    )(h, g, wq, mkb, mvb, wo)


def _xattn_sample_kernel(h_ref, g_ref, wq_ref, mk_ref, mv_ref, wo_ref, o_ref, q_s, att_s):
    b = pl.program_id(0)

    @pl.when(b == 0)
    def _():
        xn = _rms(h_ref[...], g_ref[...]).astype(BF16)
        q_s[...] = jnp.dot(xn, wq_ref[...], preferred_element_type=F32) * (MEM_HEAD_DIM ** -0.5)

    head_of_row = lax.broadcasted_iota(jnp.int32, (D_MODEL, 128), 0) // MEM_HEAD_DIM
    e_mat = (head_of_row == lax.broadcasted_iota(jnp.int32, (D_MODEL, 128), 1)).astype(BF16)
    head_of_col = lax.broadcasted_iota(jnp.int32, (128, D_MODEL), 1) // MEM_HEAD_DIM
    e_t = (head_of_col == lax.broadcasted_iota(jnp.int32, (128, D_MODEL), 0)).astype(BF16)
    q = q_s[pl.ds(b, 1), :]
    s = jnp.dot((mk_ref[...] * q).astype(BF16), e_mat, preferred_element_type=F32)
    e = jnp.exp(s - jnp.max(s, axis=0, keepdims=True))
    p = e / jnp.sum(e, axis=0, keepdims=True)
    pv = jnp.dot(p.astype(BF16), e_t, preferred_element_type=F32) * mv_ref[...]
    att_s[pl.ds(b, 1), :] = jnp.sum(pv, axis=0, keepdims=True)

    @pl.when(b == pl.num_programs(0) - 1)
    def _():
        o_ref[...] = h_ref[...] + jnp.dot(att_s[...].astype(BF16), wo_ref[...], preferred_element_type=F32)


def _xattn_sample(h, g, wq, cache_k, cache_v, wo, l):
    m = h.shape[0]
    depth, _, mt = cache_k.shape[:3]
    ck = cache_k.reshape(depth, m, mt, D_MODEL)
    cv = cache_v.reshape(depth, m, mt, D_MODEL)
    kv = pl.BlockSpec((None, None, mt, D_MODEL), lambda i: (l, i, 0, 0))
    return pl.pallas_call(
        _xattn_sample_kernel,
        grid=(m,),
        in_specs=[_full((m, D_MODEL)), _layer((1, D_MODEL), l), _layer((D_MODEL, D_MODEL), l), kv, kv,
                  _layer((D_MODEL, D_MODEL), l)],
        out_specs=_full((m, D_MODEL)),
        out_shape=jax.ShapeDtypeStruct((m, D_MODEL), F32),
        scratch_shapes=[pltpu.VMEM((m, D_MODEL), F32), pltpu.VMEM((m, D_MODEL), F32)],
        compiler_params=_cp("arbitrary"),
        name="xattn_sample",
    )(h, g, wq, ck, cv, wo)


def _mlp_kernel(h_ref, g_ref, w1_ref, w2_ref, o_ref):
    h = h_ref[...]
    xn = _rms(h, g_ref[...]).astype(BF16)
    acc = h
    for j in range(D_FF // 1024):
        a = jnp.maximum(jnp.dot(xn, w1_ref[:, j * 1024:(j + 1) * 1024], preferred_element_type=F32), 0.0)
        acc = acc + jnp.dot((a * a).astype(BF16), w2_ref[j * 1024:(j + 1) * 1024, :], preferred_element_type=F32)
    o_ref[...] = acc


def _mlp(h, g, w1, w2, l):
    m = h.shape[0]
    tm = _tile(m, 512)
    row = pl.BlockSpec((tm, D_MODEL), lambda i: (i, 0))
    return pl.pallas_call(
        _mlp_kernel,
        grid=(m // tm,),
        in_specs=[row, _layer((1, D_MODEL), l), _layer((D_MODEL, D_FF), l), _layer((D_FF, D_MODEL), l)],
        out_specs=row,
        out_shape=jax.ShapeDtypeStruct((m, D_MODEL), F32),
        compiler_params=_cp("parallel"),
        name="mlp",
    )(h, g, w1, w2)


def _norm_kernel(h_ref, g_ref, o_ref):
    o_ref[...] = _rms(h_ref[...], g_ref[...])


def _final_norm(h, g):
    m = h.shape[0]
    tm = _tile(m, 1024)
    row = pl.BlockSpec((tm, D_MODEL), lambda i: (i, 0))
    return pl.pallas_call(
        _norm_kernel,
        grid=(m // tm,),
        in_specs=[row, _full((1, D_MODEL))],
        out_specs=row,
        out_shape=jax.ShapeDtypeStruct((m, D_MODEL), F32),
        compiler_params=_cp("parallel"),
        name="final_norm",
    )(h, g)


def _block_diag(blocks):
    d, g, r, c = blocks.shape
    eye = jnp.eye(g, dtype=blocks.dtype)
    return (blocks[:, :, :, None, :] * eye[None, :, None, :, None]).reshape(d, g * r, g * c)


def kernel(x_prompt, x_sample, mem_prompt, cache_sb_k, cache_sb_v, state_s5_re, state_s5_im, state_ssm, state_conv, cache_mem_k, cache_mem_v, page_table, norm_mix, w_in, s5_lambda_re, s5_lambda_im, s5_log_dt, s5_b_re, s5_b_im, s5_c_re, s5_c_im, s5_d, s5_w_glu, s5_b_glu, sb_beta_bias, m2_conv_w, m2_conv_b, m2_dt_bias, m2_a_log, m2_d, m2_norm, w_br_s5, w_br_sb, w_br_m2, w_out, norm_mem, mem_wq, mem_wk, mem_wv, mem_wo, norm_mlp, mlp_w1, mlp_w2, norm_final):
    depth = w_in.shape[0]
    bsz, sl, _ = x_prompt.shape
    dec_b = x_sample.shape[0]
    mt = mem_prompt.shape[1]
    rep = lambda p: jnp.repeat(p, M2_HEAD_DIM, axis=-1)[:, None, :]
    row = lambda p: p[:, None, :]

    w_all = jnp.concatenate([w_in[:, :, :IN_MAIN], w_in[:, :, IN_GATE:],
                             jnp.repeat(w_in[:, :, IN_DT:IN_GATE], M2_HEAD_DIM, axis=-1)], axis=-1).astype(BF16)
    wglu, w5, wsb, wm, wo = (w.astype(BF16) for w in (s5_w_glu, w_br_s5, w_br_sb, w_br_m2, w_out))
    wq, wk, wv, wmo, w1, w2 = (w.astype(BF16) for w in (mem_wq, mem_wk, mem_wv, mem_wo, mlp_w1, mlp_w2))
    g_mix, g_mem, g_mlp = row(norm_mix), row(norm_mem), row(norm_mlp)
    s5d, bglu, cb = row(s5_d), row(s5_b_glu), row(m2_conv_b)
    dtb, a_neg, m2d, m2n = rep(m2_dt_bias), rep(-jnp.exp(m2_a_log.astype(F32))), rep(m2_d), row(m2_norm)
    tri_t = (jnp.arange(SB_TILE)[:, None] <= jnp.arange(SB_TILE)[None, :]).astype(BF16)
    bias_col = jnp.broadcast_to(sb_beta_bias[:, :, None], (depth, SB_HEADS, 128))
    h0r = state_s5_re.reshape(depth, dec_b, 2048)
    h0i = state_s5_im.reshape(depth, dec_b, 2048)
    mem2 = mem_prompt.reshape(bsz * mt, D_MODEL)

    hp = x_prompt.reshape(bsz * sl, D_MODEL)
    hs = x_sample.reshape(dec_b, D_MODEL)
    outs = [[] for _ in range(14)]
    flat = lambda p: p.reshape((depth * S5_GROUPS,) + p.shape[2:])
    wcat, poutr, pouti, pw, bbr, bbi = _s5_prep(flat(s5_lambda_re), flat(s5_lambda_im), flat(s5_log_dt),
                                                 flat(s5_b_re), flat(s5_b_im), flat(s5_c_re), flat(s5_c_im))
    power = lambda part, k: pw[:, part, k].reshape(depth, 1, 2048)
    a16r, a16i, a1r, a1i = power(0, S5_T), power(1, S5_T), power(0, 1), power(1, 1)
    per_layer = lambda p: p.reshape((depth, S5_GROUPS) + p.shape[1:])
    bbr_d = _block_diag(jnp.swapaxes(per_layer(bbr[:, :, :S5_CH]), 2, 3)).astype(BF16)
    bbi_d = _block_diag(jnp.swapaxes(per_layer(bbi[:, :, :S5_CH]), 2, 3)).astype(BF16)
    ccr_d = _block_diag(jnp.swapaxes(s5_c_re, 2, 3)).astype(BF16)
    cci_d = _block_diag(jnp.swapaxes(s5_c_im, 2, 3)).astype(BF16)

    for l in range(depth):

        u4, kt, vt, vtb, q2, k2, zg, xbc, gates, dtr = _in_proj_prompt(hp, g_mix, w_all, sb_beta_bias, l, bsz)
        r3 = lambda t: t.reshape(bsz, sl, t.shape[-1])
        y5, s5r, s5i = _s5_prompt(u4.reshape(bsz, sl * 4, 128), wcat, poutr, pouti, a16r, a16i,
                                  s5d, wglu, bglu, l)
        ysb = _sb_prompt(r3(q2), r3(k2), vtb, tri_t)
        ym, ssm, cvs = _ssd_prompt(r3(xbc), r3(dtr), r3(zg), m2_conv_w, cb, dtb, a_neg, m2d, m2n, l)
        hp = _merge(hp, y5.reshape(bsz * sl * 4, 128), ysb.reshape(bsz * sl, SB_WIDTH),
                    ym.reshape(bsz * sl, M2_INNER), gates, w5, wsb, wm, wo, l)
        mk, mv, mkb, mvb = _mem_kv(mem2, wk, wv, l)
        hp = _xattn_prompt(hp.reshape(bsz, sl, D_MODEL), g_mem, wq, mkb.reshape(bsz, mt, D_MODEL),
                           mvb.reshape(bsz, mt, D_MODEL), wmo, l).reshape(bsz * sl, D_MODEL)
        hp = _mlp(hp, g_mlp, w1, w2, l)
        for lst, val in zip(outs[:8], (
                kt, vt, s5r.reshape(bsz, S5_GROUPS, S5_STATE), s5i.reshape(bsz, S5_GROUPS, S5_STATE),
                ssm.reshape(bsz, M2_HEADS, M2_HEAD_DIM, M2_STATE), cvs,
                mk.reshape(bsz, mt, MEM_HEADS, MEM_HEAD_DIM), mv.reshape(bsz, mt, MEM_HEADS, MEM_HEAD_DIM))):
            lst.append(val)

        u, k, v, qs, zg, xbc, gates, dtr = _in_proj_sample(hs, g_mix, w_all, l)
        y5, s5r, s5i = _s5_step(u, h0r, h0i, a1r, a1i, bbr_d, bbi_d, ccr_d, cci_d, s5d, wglu, bglu, l)
        ysb = _sb_decode(qs, bias_col, cache_sb_k, cache_sb_v, page_table, l)
        ym, ssm, cvs = _ssd_step(xbc, state_conv, dtr, zg, state_ssm, m2_conv_w, cb, dtb, a_neg, m2d, m2n, l)
        hs = _merge(hs, y5, ysb, ym, gates, w5, wsb, wm, wo, l)
        hs = _xattn_sample(hs, g_mem, wq, cache_mem_k, cache_mem_v, wmo, l)
        hs = _mlp(hs, g_mlp, w1, w2, l)
        for lst, val in zip(outs[8:], (
                k.reshape(dec_b, 1, SB_HEADS, SB_HEAD_DIM), v.reshape(dec_b, 1, SB_HEADS, SB_HEAD_DIM),
                s5r.reshape(dec_b, S5_GROUPS, S5_STATE), s5i.reshape(dec_b, S5_GROUPS, S5_STATE),
                ssm.reshape(dec_b, M2_HEADS, M2_HEAD_DIM, M2_STATE), cvs)):
            lst.append(val)

    g_fin = norm_final[None, :]
    y_prompt = _final_norm(hp, g_fin).reshape(bsz, sl, D_MODEL)
    y_sample = _final_norm(hs, g_fin).reshape(dec_b, 1, D_MODEL)
    stacked = [jnp.stack(o) for o in outs]
    for i in (0, 1):
        stacked[i] = jnp.transpose(stacked[i].reshape(depth, bsz, SB_HEADS, SB_HEAD_DIM, sl), (0, 1, 4, 2, 3))
    return (y_prompt, y_sample) + tuple(stacked)
```

```python
import functools
import math

import jax
import jax.numpy as jnp
from jax import lax
from jax.experimental import pallas as pl
from jax.experimental.pallas import tpu as pltpu

F32 = jnp.float32
BF16 = jnp.bfloat16

D_MODEL = 1024
PAGE_SIZE = 128
S5_CH = 16
S5_WIDTH = 512
S5_GROUPS = 32
S5_STATE = 64
S5_LBLK = 2048
S5_T = 16
SB_WIDTH = 512
SB_HEADS = 8
SB_HEAD_DIM = 64
SB_TILE = 256
SB_QTILE = 1024
SB_UNROLL = 4
M2_INNER = 512
M2_HEADS = 8
M2_HEAD_DIM = 64
M2_GROUPS = 2
M2_STATE = 128
M2_CONV = 4
M2_CONV_DIM = 1024
M2_Q = 256
MEM_HEADS = 4
MEM_HEAD_DIM = 256
D_FF = 4096
EPS = 1e-6
LOG2E = 1.4426950408889634
SOFTPLUS_LINEAR_ABOVE = 30.0
IN_MAIN = 3584
IN_DT = 3584
IN_GATE = 3592
VMEM_LIMIT_V7X = 56 * 1024 * 1024


def _cp(*sem):
    return pltpu.CompilerParams(dimension_semantics=sem, vmem_limit_bytes=VMEM_LIMIT_V7X)


def _tile(m, pref):
    t = min(m, pref)
    while m % t:
        t -= 8
    return t


def _full(shape):
    nd = len(shape)
    return pl.BlockSpec(shape, lambda *_: (0,) * nd)


def _layer(shape, l):
    nd = len(shape)
    return pl.BlockSpec((None,) + tuple(shape), lambda *_: (l,) + (0,) * nd)


def _rms(x, g):
    return x * lax.rsqrt(jnp.mean(x * x, axis=-1, keepdims=True) + EPS) * g


def _bdot(a, b):
    return jnp.dot(a.astype(BF16), b.astype(BF16), preferred_element_type=F32)


def _dot_nt(a, b):
    return lax.dot_general(a, b, (((1,), (1,)), ((), ())), preferred_element_type=F32)


def _dot_tn(a, b):
    return lax.dot_general(a, b, (((0,), (0,)), ((), ())), preferred_element_type=F32)


def _softplus(z):
    return jnp.maximum(z, 0.0) + jnp.log(1.0 + jnp.exp(-jnp.abs(z)))


def _sigmoid(z):
    return 1.0 / (1.0 + jnp.exp(-z))


def _silu(z):
    return z * _sigmoid(z)


def _gelu_tanh(x):
    return 0.5 * x * (1.0 + jnp.tanh(math.sqrt(2.0 / math.pi) * (x + 0.044715 * (x * x * x))))


def _in_proj_common(x_ref, g_ref, w_ref, zg_ref, xbc_ref, gate_ref, dt_ref):
    xn = _rms(x_ref[...], g_ref[...]).astype(BF16)

    def mm(lo, hi):
        return jnp.dot(xn, w_ref[:, lo:hi], preferred_element_type=F32)

    zg_ref[...] = mm(2048, 2560)
    xbc_ref[...] = mm(2560, 3584)
    gate_ref[...] = mm(3584, 6656).astype(gate_ref.dtype)
    dt_ref[...] = mm(6656, 7168)
    return mm


def _in_proj_prompt_kernel(l, bias_ref, x_ref, g_ref, w_ref, u4_ref, kt_ref, vt_ref, vtb_ref, q2_ref, k2_ref,
                           zg_ref, xbc_ref, gate_ref, dt_ref):
    mm = _in_proj_common(x_ref, g_ref, w_ref, zg_ref, xbc_ref, gate_ref, dt_ref)
    tm = x_ref.shape[0]
    u = mm(0, 512)
    for j in range(4):
        u4_ref[pl.ds(j, tm, stride=4), :] = u[:, j * 128:(j + 1) * 128]
    q = mm(512, 1024) * (SB_HEAD_DIM ** -0.5)
    k = mm(1024, 1536)
    v = mm(1536, 2048)
    kt_ref[...] = k.T
    vt_ref[...] = v.T
    vtb_ref[...] = v.astype(BF16).T
    lane = lax.broadcasted_iota(jnp.int32, (tm, 128), 1)
    for h in range(SB_HEADS):
        sl = slice((h // 2) * 128, (h // 2 + 1) * 128)
        qh, kh = q[:, sl], k[:, sl]
        if h % 2:
            qh, kh = pltpu.roll(qh, 64, 1), pltpu.roll(kh, 64, 1)
        extra = lane == SB_HEAD_DIM
        q2_ref[:, h * 128:(h + 1) * 128] = jnp.where(
            lane < SB_HEAD_DIM, qh, jnp.where(extra, bias_ref[l, h], 0.0)).astype(BF16)
        k2_ref[:, h * 128:(h + 1) * 128] = jnp.where(
            lane < SB_HEAD_DIM, kh, jnp.where(extra, 1.0, 0.0)).astype(BF16)


def _in_proj_prompt(x, g, w_all, bias, l, bsz):
    m = x.shape[0]
    sl = m // bsz
    tm = _tile(sl, 256)
    per_b = sl // tm
    row = lambda n: pl.BlockSpec((tm, n), lambda i: (i, 0))
    tr = pl.BlockSpec((None, 512, tm), lambda i: (i // per_b, 0, i % per_b))
    outs = [(1024, BF16), (1024, BF16), (512, F32), (1024, F32), (3072, BF16), (512, F32)]
    return pl.pallas_call(
        functools.partial(_in_proj_prompt_kernel, l),
        grid=(m // tm,),
        in_specs=[pl.BlockSpec(memory_space=pltpu.SMEM),
                  row(D_MODEL), _layer((1, D_MODEL), l), _layer((D_MODEL, 7168), l)],
        out_specs=[pl.BlockSpec((tm * 4, 128), lambda i: (i, 0)), tr, tr, tr] + [row(n) for n, _ in outs],
        out_shape=[jax.ShapeDtypeStruct((m * 4, 128), F32),
                   jax.ShapeDtypeStruct((bsz, 512, sl), F32), jax.ShapeDtypeStruct((bsz, 512, sl), F32),
                   jax.ShapeDtypeStruct((bsz, 512, sl), BF16)]
                  + [jax.ShapeDtypeStruct((m, n), dt) for n, dt in outs],
        compiler_params=_cp("parallel"),
        name="in_proj",
    )(bias, x, g, w_all)


def _in_proj_sample_kernel(x_ref, g_ref, w_ref, u_ref, k_ref, v_ref, q_ref, zg_ref, xbc_ref, gate_ref, dt_ref):
    mm = _in_proj_common(x_ref, g_ref, w_ref, zg_ref, xbc_ref, gate_ref, dt_ref)
    u_ref[...] = mm(0, 512)
    q_ref[...] = mm(512, 1024) * (SB_HEAD_DIM ** -0.5)
    k_ref[...] = mm(1024, 1536)
    v_ref[...] = mm(1536, 2048)


def _in_proj_sample(x, g, w_all, l):
    m = x.shape[0]
    outs = [512, 512, 512, 512, 512, 1024, 3072, 512]
    return pl.pallas_call(
        _in_proj_sample_kernel,
        grid=(1,),
        in_specs=[_full((m, D_MODEL)), _layer((1, D_MODEL), l), _layer((D_MODEL, 7168), l)],
        out_specs=[_full((m, n)) for n in outs],
        out_shape=[jax.ShapeDtypeStruct((m, n), F32) for n in outs],
        compiler_params=_cp("arbitrary"),
        name="in_proj_step",
    )(x, g, w_all)


def _s5_prep_kernel(lr_row, li_row, lr_col, li_col, ldt_ref, btr_ref, bti_ref, ctr_ref, cti_ref,
                    wcat_ref, poutr_ref, pouti_ref, pw_ref, bbr_ref, bbi_ref):
    dt = jnp.exp(ldt_ref[0])
    lr, li = lr_row[0], li_row[0]
    kk = lax.broadcasted_iota(jnp.int32, (24, 1), 0).astype(F32)
    mag = jnp.exp(kk * (lr * dt))
    ang = kk * (li * dt)
    p_re, p_im = mag * jnp.cos(ang), mag * jnp.sin(ang)
    pw_ref[0, 0] = p_re
    pw_ref[0, 1] = p_im

    def rep(p, lo):
        return jnp.broadcast_to(p[lo:lo + S5_T][:, None, :], (S5_T, S5_CH, S5_STATE)).reshape(256, S5_STATE)

    c_re, c_im = ctr_ref[0], cti_ref[0]
    pr0, pi0 = rep(p_re, 0), rep(p_im, 0)
    l_re = c_re * pr0 - c_im * pi0
    l_im = c_re * pi0 + c_im * pr0
    pr1, pi1 = rep(p_re, 1), rep(p_im, 1)
    poutr_ref[0] = (c_re * pr1 - c_im * pi1).astype(BF16)
    pouti_ref[0] = (-(c_re * pi1 + c_im * pr1)).astype(BF16)

    lrc, lic = lr_col[0], li_col[0]
    m1 = jnp.exp(lrc * dt)
    a_re, a_im = m1 * jnp.cos(lic * dt), m1 * jnp.sin(lic * dt)
    den = lrc * lrc + lic * lic
    f_re = ((a_re - 1.0) * lrc + a_im * lic) / den
    f_im = (a_im * lrc - (a_re - 1.0) * lic) / den
    b_re, b_im = btr_ref[0], bti_ref[0]
    bb_re = f_re * b_re - f_im * b_im
    bb_im = f_re * b_im + f_im * b_re
    bbr_ref[0] = bb_re
    bbi_ref[0] = bb_im

    hp = lax.Precision.HIGHEST
    kt = (jnp.dot(l_re, bb_re, precision=hp, preferred_element_type=F32)
          - jnp.dot(l_im, bb_im, precision=hp, preferred_element_type=F32))
    lane_s = lax.broadcasted_iota(jnp.int32, (1, 256), 1) // S5_CH
    toep = jnp.where(lane_s == 0, kt, 0.0)
    for s in range(1, S5_T):
        sh = jnp.concatenate([jnp.zeros((S5_CH * s, 256), F32), kt[:256 - S5_CH * s]], axis=0)
        toep = jnp.where(lane_s == s, sh, toep)

    ks = (S5_T - 1 - lane_s).astype(F32)
    magc = jnp.exp(ks * (lrc * dt))
    angc = ks * (lic * dt)
    q_re, q_im = magc * jnp.cos(angc), magc * jnp.sin(angc)
    wcat_ref[0, 0:256] = toep.astype(BF16)
    wcat_ref[0, 256:320] = (q_re * bb_re - q_im * bb_im).astype(BF16)
    wcat_ref[0, 320:384] = (q_re * bb_im + q_im * bb_re).astype(BF16)


def _s5_prep(lam_re, lam_im, log_dt, b_re, b_im, c_re, c_im):
    g = lam_re.shape[0]
    per = lambda *shape: pl.BlockSpec((1,) + shape, lambda i: (i,) + (0,) * len(shape))
    tile_b = lambda b: jnp.tile(b, (1, 1, S5_T))
    tile_c = lambda c: jnp.tile(c, (1, S5_T, 1))
    return pl.pallas_call(
        _s5_prep_kernel,
        grid=(g,),
        in_specs=[per(1, 64), per(1, 64), per(64, 1), per(64, 1), per(1, 1),
                  per(64, 256), per(64, 256), per(256, 64), per(256, 64)],
        out_specs=[per(384, 256), per(256, 64), per(256, 64), per(2, 24, 64), per(64, 256), per(64, 256)],
        out_shape=[jax.ShapeDtypeStruct((g, 384, 256), BF16),
                   jax.ShapeDtypeStruct((g, 256, 64), BF16),
                   jax.ShapeDtypeStruct((g, 256, 64), BF16),
                   jax.ShapeDtypeStruct((g, 2, 24, 64), F32),
                   jax.ShapeDtypeStruct((g, 64, 256), F32),
                   jax.ShapeDtypeStruct((g, 64, 256), F32)],
        compiler_params=_cp("parallel"),
        name="s5_prep",
    )(lam_re[:, None, :], lam_im[:, None, :], lam_re[:, :, None], lam_im[:, :, None],
      log_dt[:, None, None], tile_b(b_re), tile_b(b_im), tile_c(c_re), tile_c(c_im))


def _rows(u_ref, s, j, n):
    return u_ref[0, pl.ds(s * 4 + j, n, stride=4 * S5_T), :]


def _s5_prompt_kernel(u_ref, wcat_ref, poutr_ref, pouti_ref, ar_ref, ai_ref, d_ref, wglu_ref, bglu_ref,
                      y_ref, sr_ref, si_ref, v_s, yt_s, hr_s, hi_s, cr_s, ci_s):
    n = v_s.shape[2]
    lb = pl.program_id(1)

    @pl.when(lb == 0)
    def _():
        cr_s[...] = jnp.zeros_like(cr_s)
        ci_s[...] = jnp.zeros_like(ci_s)

    for s in range(S5_T):
        for j in range(4):
            v_s[s, j * 128:(j + 1) * 128, :] = _rows(u_ref, s, j, n).T.astype(BF16)

    for gp in range(S5_GROUPS // 2):
        res = []
        for g in (2 * gp, 2 * gp + 1):
            vg = v_s[:, g * S5_CH:(g + 1) * S5_CH, :].reshape(S5_T * S5_CH, n)
            r = jnp.dot(wcat_ref[g], vg, preferred_element_type=F32)
            yt_s[g] = r[0:256]
            res.append(r)
        hr_s[:, gp * 128:(gp + 1) * 128] = jnp.concatenate([res[0][256:320], res[1][256:320]], axis=0).T
        hi_s[:, gp * 128:(gp + 1) * 128] = jnp.concatenate([res[0][320:384], res[1][320:384]], axis=0).T

    a_re, a_im = ar_ref[...], ai_ref[...]

    def step(i, carry):
        h_re, h_im = carry
        s_re = hr_s[pl.ds(i, 1), :]
        s_im = hi_s[pl.ds(i, 1), :]
        hr_s[pl.ds(i, 1), :] = h_re
        hi_s[pl.ds(i, 1), :] = h_im
        return (a_re * h_re - a_im * h_im + s_re, a_re * h_im + a_im * h_re + s_im)

    h_re, h_im = lax.fori_loop(0, n, step, (cr_s[...], ci_s[...]))
    cr_s[...] = h_re
    ci_s[...] = h_im
    sr_ref[0] = h_re
    si_ref[0] = h_im

    for gp in range(S5_GROUPS // 2):
        ht_re = hr_s[:, gp * 128:(gp + 1) * 128].T.astype(BF16)
        ht_im = hi_s[:, gp * 128:(gp + 1) * 128].T.astype(BF16)
        for j, g in enumerate((2 * gp, 2 * gp + 1)):
            yt_s[g] += (jnp.dot(poutr_ref[g], ht_re[64 * j:64 * j + 64], preferred_element_type=F32)
                        + jnp.dot(pouti_ref[g], ht_im[64 * j:64 * j + 64], preferred_element_type=F32))

    for t in range(S5_T):
        yt = yt_s[:, t * S5_CH:(t + 1) * S5_CH, :].reshape(S5_WIDTH, n).T
        y = yt + d_ref[...] * jnp.concatenate([_rows(u_ref, t, j, n) for j in range(4)], axis=1)
        gl = _gelu_tanh(y)
        o = gl * _sigmoid(jnp.dot(gl.astype(BF16), wglu_ref[...], preferred_element_type=F32) + bglu_ref[...])
        for j in range(4):
            y_ref[0, pl.ds(t * 4 + j, n, stride=4 * S5_T), :] = o[:, j * 128:(j + 1) * 128]


def _s5_prompt(u4, wcat, poutr, pouti, a16r, a16i, d, wglu, bglu, l):
    b, sl = u4.shape[0], u4.shape[1] // 4
    lblk = _tile(sl, S5_LBLK)
    n = lblk // S5_T
    grp = lambda r, c: pl.BlockSpec((S5_GROUPS, r, c), lambda i, j: (l, 0, 0))
    return pl.pallas_call(
        _s5_prompt_kernel,
        grid=(b, sl // lblk),
        in_specs=[pl.BlockSpec((1, lblk * 4, 128), lambda i, j: (i, j, 0)),
                  grp(384, 256), grp(256, 64), grp(256, 64),
                  _layer((1, 2048), l), _layer((1, 2048), l),
                  _layer((1, S5_WIDTH), l), _layer((S5_WIDTH, S5_WIDTH), l), _layer((1, S5_WIDTH), l)],
        out_specs=[pl.BlockSpec((1, lblk * 4, 128), lambda i, j: (i, j, 0)),
                   pl.BlockSpec((1, 1, 2048), lambda i, j: (i, 0, 0)),
                   pl.BlockSpec((1, 1, 2048), lambda i, j: (i, 0, 0))],
        out_shape=[jax.ShapeDtypeStruct((b, sl * 4, 128), F32),
                   jax.ShapeDtypeStruct((b, 1, 2048), F32),
                   jax.ShapeDtypeStruct((b, 1, 2048), F32)],
        scratch_shapes=[pltpu.VMEM((S5_T, S5_WIDTH, n), BF16),
                        pltpu.VMEM((S5_GROUPS, 256, n), F32),
                        pltpu.VMEM((n, 2048), F32), pltpu.VMEM((n, 2048), F32),
                        pltpu.VMEM((1, 2048), F32), pltpu.VMEM((1, 2048), F32)],
        compiler_params=_cp("parallel", "arbitrary"),
        name="s5_prompt",
    )(u4, wcat, poutr, pouti, a16r, a16i, d, wglu, bglu)


def _s5_step_kernel(u_ref, h0r_ref, h0i_ref, ar_ref, ai_ref, bbr_ref, bbi_ref, ccr_ref, cci_ref,
                    d_ref, wglu_ref, bglu_ref, y_ref, hr_ref, hi_ref):
    u = u_ref[...]
    ub = u.astype(BF16)
    a_re, a_im = ar_ref[...], ai_ref[...]
    h0r, h0i = h0r_ref[...], h0i_ref[...]
    h_re = a_re * h0r - a_im * h0i + jnp.dot(ub, bbr_ref[...], preferred_element_type=F32)
    h_im = a_re * h0i + a_im * h0r + jnp.dot(ub, bbi_ref[...], preferred_element_type=F32)
    hr_ref[...] = h_re
    hi_ref[...] = h_im
    y = (jnp.dot(h_re.astype(BF16), ccr_ref[...], preferred_element_type=F32)
         - jnp.dot(h_im.astype(BF16), cci_ref[...], preferred_element_type=F32) + d_ref[...] * u)
    gl = _gelu_tanh(y)
    y_ref[...] = gl * _sigmoid(jnp.dot(gl.astype(BF16), wglu_ref[...], preferred_element_type=F32) + bglu_ref[...])


def _s5_step(u, h0r, h0i, a1r, a1i, bbr, bbi, ccr, cci, d, wglu, bglu, l):
    m = u.shape[0]
    return pl.pallas_call(
        _s5_step_kernel,
        grid=(1,),
        in_specs=[_full((m, S5_WIDTH)), _layer((m, 2048), l), _layer((m, 2048), l),
                  _layer((1, 2048), l), _layer((1, 2048), l),
                  _layer((S5_WIDTH, 2048), l), _layer((S5_WIDTH, 2048), l),
                  _layer((2048, S5_WIDTH), l), _layer((2048, S5_WIDTH), l),
                  _layer((1, S5_WIDTH), l), _layer((S5_WIDTH, S5_WIDTH), l), _layer((1, S5_WIDTH), l)],
        out_specs=[_full((m, S5_WIDTH)), _full((m, 2048)), _full((m, 2048))],
        out_shape=[jax.ShapeDtypeStruct((m, S5_WIDTH), F32),
                   jax.ShapeDtypeStruct((m, 2048), F32), jax.ShapeDtypeStruct((m, 2048), F32)],
        compiler_params=_cp("arbitrary"),
        name="s5_step",
    )(u, h0r, h0i, a1r, a1i, bbr, bbi, ccr, cci, d, wglu, bglu)


def _sb_prompt_kernel(q_ref, k_ref, vt_ref, tri_ref, o_ref, acc_s, c_s):
    tq, tk, tc = q_ref.shape[1], SB_TILE, 256
    per = tq // tk
    qi = pl.program_id(2)
    chunks = [(hh, a) for hh in range(2) for a in range(tq // tc)]
    qs = [q_ref[0, a * tc:(a + 1) * tc, hh * 128:(hh + 1) * 128] for hh, a in chunks]
    key = lax.broadcasted_iota(jnp.int32, (tk, tc), 0)
    qry = lax.broadcasted_iota(jnp.int32, (tk, tc), 1)
    acc_s[...] = jnp.zeros_like(acc_s)
    c_s[...] = jnp.zeros_like(c_s)

    def block(kb, off):
        start = pl.multiple_of(kb * tk, tk)
        kblk = k_ref[0, pl.ds(start, tk), :]
        vt = vt_ref[0, :, pl.ds(start, tk)]
        live, masks = [], {}
        for i, (hh, a) in enumerate(chunks):
            if off is not None and off >= (a + 1) * tc - 1:
                continue
            live.append(i)
            if off is not None and off + tk - 1 >= a * tc:
                masks[i] = key + (off - a * tc) < qry
        def logits(i):
            hh = chunks[i][0]
            return _dot_nt(kblk[:, hh * 128:(hh + 1) * 128], qs[i])

        zs, rs = {live[0]: logits(live[0])}, {}
        for n_done, i in enumerate(live):
            if n_done + 1 < len(live):
                zs[live[n_done + 1]] = logits(live[n_done + 1])
            z = zs[i]
            sp = jnp.where(z > SOFTPLUS_LINEAR_ABOVE, z, jnp.log(1.0 + jnp.exp2(z * LOG2E)))
            if i in masks:
                sp = jnp.where(masks[i], sp, 0.0)
            rs[i] = jnp.dot(tri_ref[...], sp.astype(BF16), preferred_element_type=F32)
        for i in live:
            w = jnp.exp(zs[i] - rs[i])
            if i in masks:
                w = jnp.where(masks[i], w, 0.0)
            c = c_s[i]
            acc_s[i] += jnp.exp(-c) * jnp.dot(vt, w.astype(BF16), preferred_element_type=F32)
            c_s[i] = c + rs[i][0:1, :]

    for d in range(per - 1, -1, -1):
        block(qi * per + d, d * tk)
    n = qi * per
    trips = n // SB_UNROLL

    @pl.loop(0, trips)
    def _(j):
        for d in range(SB_UNROLL):
            block(n - 1 - j * SB_UNROLL - d, None)

    assert per % SB_UNROLL == 0 or SB_UNROLL == 2 * per
    if per % SB_UNROLL:
        @pl.when(n - trips * SB_UNROLL > 0)
        def _():
            for d in range(per - 1, -1, -1):
                block(d, None)

    chan = lax.broadcasted_iota(jnp.int32, (128, tc), 0)
    for a in range(tq // tc):
        o_ref[0, a * tc:(a + 1) * tc, :] = jnp.where(
            chan < SB_HEAD_DIM, acc_s[a], acc_s[tq // tc + a]).T.astype(o_ref.dtype)


def _sb_prompt(q2, k2, vtb, tri_t):
    b, _, sl = vtb.shape
    tq = _tile(sl, SB_QTILE)
    return pl.pallas_call(
        _sb_prompt_kernel,
        grid=(b, SB_HEADS // 2, sl // tq),
        in_specs=[pl.BlockSpec((1, tq, 256), lambda i, p, j: (i, j, p)),
                  pl.BlockSpec((1, sl, 256), lambda i, p, j: (i, 0, p)),
                  pl.BlockSpec((1, 128, sl), lambda i, p, j: (i, p, 0)),
                  _full((SB_TILE, SB_TILE))],
        out_specs=pl.BlockSpec((1, tq, 128), lambda i, p, j: (i, j, p)),
        out_shape=jax.ShapeDtypeStruct((b, sl, SB_WIDTH), BF16),
        scratch_shapes=[pltpu.VMEM((2 * tq // 256, 128, 256), F32), pltpu.VMEM((2 * tq // 256, 1, 256), F32)],
        compiler_params=_cp("parallel", "parallel", "arbitrary"),
        name="sb_prompt",
    )(q2, k2, vtb, tri_t)


def _split_bf16(x):
    hi = x.astype(BF16)
    return hi, (x - hi.astype(F32)).astype(BF16)


def _sb_decode_kernel(npg, pt_ref, q_ref, bias_ref, tri_ref, later_ref, *refs):
    k_refs, v_refs = refs[:npg], refs[npg:2 * npg]
    o_ref, qc_s, acc_s, c_s = refs[2 * npg:]
    j = pl.program_id(1)

    @pl.when(j == 0)
    def _():
        qc_s[...] = jnp.broadcast_to(q_ref[0], (128, SB_WIDTH)).T
        acc_s[...] = jnp.zeros_like(acc_s)
        c_s[...] = jnp.zeros_like(c_s)

    qc = qc_s[...]
    bias = jnp.concatenate([bias_ref[...]] * npg, axis=0)
    z = jnp.concatenate(
        [(k_refs[i][...] * qc).reshape(SB_HEADS, SB_HEAD_DIM, PAGE_SIZE).sum(axis=1) for i in range(npg)],
        axis=0) + bias
    lk = -_softplus(z)
    hi, lo = _split_bf16(lk)
    r_loc = jnp.dot(jnp.concatenate([hi, lo], axis=1), tri_ref[...], preferred_element_type=F32)
    tot = jnp.broadcast_to(r_loc[:, 0:1], r_loc.shape)
    hi, lo = _split_bf16(tot)
    c_in = c_s[...]
    r = (r_loc + jnp.dot(later_ref[...], jnp.concatenate([hi, lo], axis=0), preferred_element_type=F32)
         + jnp.concatenate([c_in] * npg, axis=0))
    w = jnp.exp(z + r)
    c_s[...] = r[0:SB_HEADS] - r_loc[0:SB_HEADS] + tot[0:SB_HEADS]
    acc = acc_s[...]
    for i in range(npg):
        w_rows = jnp.broadcast_to(w[i * SB_HEADS:(i + 1) * SB_HEADS][:, None, :],
                                  (SB_HEADS, SB_HEAD_DIM, PAGE_SIZE)).reshape(SB_WIDTH, PAGE_SIZE)
        acc = acc + v_refs[i][...] * w_rows
    acc_s[...] = acc

    @pl.when(j == pl.num_programs(1) - 1)
    def _():
        o_ref[0] = jnp.sum(acc.T, axis=0, keepdims=True)


def _sb_decode(q, bias_col, cache_k, cache_v, page_table, l):
    m = q.shape[0]
    n_pages = page_table.shape[1]
    npg = 16
    while n_pages % npg:
        npg //= 2
    steps = n_pages // npg
    rows = npg * SB_HEADS

    def page(i):
        return pl.BlockSpec((None, None, SB_WIDTH, PAGE_SIZE),
                            lambda b, j, pt: (l, pt[b, (steps - 1 - j) * npg + i], 0, 0))

    const = lambda shape: pl.BlockSpec(shape, lambda b, j, pt: (0,) * len(shape))
    grid_spec = pltpu.PrefetchScalarGridSpec(
        num_scalar_prefetch=1,
        grid=(m, steps),
        in_specs=[pl.BlockSpec((1, 1, SB_WIDTH), lambda b, j, pt: (b, 0, 0)),
                  pl.BlockSpec((None, SB_HEADS, 128), lambda b, j, pt: (l, 0, 0)),
                  const((2 * PAGE_SIZE, PAGE_SIZE)), const((rows, 2 * rows))]
                 + [page(i) for i in range(npg)] * 2,
        out_specs=pl.BlockSpec((1, 1, SB_WIDTH), lambda b, j, pt: (b, 0, 0)),
        scratch_shapes=[pltpu.VMEM((SB_WIDTH, 128), F32), pltpu.VMEM((SB_WIDTH, PAGE_SIZE), F32),
                        pltpu.VMEM((SB_HEADS, 128), F32)],
    )
    depth, n_pool = cache_k.shape[:2]
    ck = jnp.transpose(cache_k, (0, 1, 3, 4, 2)).reshape(depth, n_pool, SB_WIDTH, PAGE_SIZE)
    cv = jnp.transpose(cache_v, (0, 1, 3, 4, 2)).reshape(depth, n_pool, SB_WIDTH, PAGE_SIZE)
    tri = (jnp.arange(2 * PAGE_SIZE)[:, None] % PAGE_SIZE >= jnp.arange(PAGE_SIZE)[None, :]).astype(BF16)
    ridx = jnp.arange(rows)
    later = ((ridx[None, :] % SB_HEADS == ridx[:, None] % SB_HEADS)
             & (ridx[None, :] // SB_HEADS > ridx[:, None] // SB_HEADS))
    later = jnp.concatenate([later, later], axis=1).astype(BF16)
    out = pl.pallas_call(
        functools.partial(_sb_decode_kernel, npg),
        grid_spec=grid_spec,
        out_shape=jax.ShapeDtypeStruct((m, 1, SB_WIDTH), F32),
        compiler_params=_cp("parallel", "arbitrary"),
        name="sb_decode",
    )(page_table, q.reshape(m, 1, SB_WIDTH), bias_col, tri, later, *([ck] * npg), *([cv] * npg))
    return out.reshape(m, SB_WIDTH)


def _cumsum_rows(x):
    n = x.shape[0]
    row = lax.broadcasted_iota(jnp.int32, (n, 1), 0)
    k = 1
    while k < n:
        x = x + jnp.where(row >= k, pltpu.roll(x, k, 0), 0.0)
        k *= 2
    return x


def _ssd_prompt_kernel(xbc_ref, dt_ref, zg_ref, cw_ref, cb_ref, dtb_ref, a_ref, d_ref, nrm_ref,
                       y_ref, st_ref, cv_ref, prev_s, st_s):
    q = xbc_ref.shape[1]
    c = pl.program_id(1)

    @pl.when(c == 0)
    def _():
        prev_s[...] = jnp.zeros_like(prev_s)
        st_s[...] = jnp.zeros_like(st_s)

    u = xbc_ref[0]
    prev = prev_s[...]
    row8 = lax.broadcasted_iota(jnp.int32, (8, 1), 0)
    conv = cb_ref[...] + cw_ref[M2_CONV - 1:M2_CONV, :] * u
    for k in range(1, M2_CONV):
        ru = pltpu.roll(u, k, 0)
        top = jnp.where(row8 < k, pltpu.roll(prev, k, 0), ru[0:8])
        conv = conv + cw_ref[M2_CONV - 1 - k:M2_CONV - k, :] * jnp.concatenate([top, ru[8:]], axis=0)
    prev_s[...] = u[q - 8:q]
    xc = _silu(conv)
    xs = xc[:, 0:M2_INNER]
    dt = _softplus(dt_ref[0] + dtb_ref[...])
    acum = _cumsum_rows(dt * a_ref[...])
    a_last = acum[q - 1:q, :]
    xdt = xs * dt
    xw = (xdt * jnp.exp(a_last - acum)).astype(BF16)
    xdt_b = xdt.astype(BF16)
    e_acum = jnp.exp(acum)
    rows = lax.broadcasted_iota(jnp.int32, (q, q), 0)
    cols = lax.broadcasted_iota(jnp.int32, (q, q), 1)
    causal = cols <= rows
    lane = lax.broadcasted_iota(jnp.int32, (q, 128), 1)
    y_parts = []
    for g in range(M2_GROUPS):
        bm = xc[:, M2_INNER + g * M2_STATE:M2_INNER + (g + 1) * M2_STATE].astype(BF16)
        cm = xc[:, M2_INNER + (M2_GROUPS + g) * M2_STATE:M2_INNER + (M2_GROUPS + g + 1) * M2_STATE].astype(BF16)
        cb = _dot_nt(cm, bm)
        st = st_s[g]
        y_off = jnp.dot(cm, st.astype(BF16), preferred_element_type=F32) * e_acum[:, g * 256:(g + 1) * 256]
        st_s[g] = st * jnp.exp(a_last[:, g * 256:(g + 1) * 256]) + _dot_tn(bm, xw[:, g * 256:(g + 1) * 256])
        for pp in range(2):
            p = 2 * g + pp
            a_pair = acum[:, p * 128:(p + 1) * 128]
            a_t = a_pair.T
            yd = []
            for hh in range(2):
                seg = a_pair[:, 64 * hh:64 * hh + 1] - a_t[64 * hh:64 * hh + 1, :]
                mix = jnp.where(causal, cb * jnp.exp(jnp.minimum(seg, 0.0)), 0.0).astype(BF16)
                yd.append(jnp.dot(mix, xdt_b[:, p * 128:(p + 1) * 128], preferred_element_type=F32))
            y_parts.append(jnp.where(lane < M2_HEAD_DIM, yd[0], yd[1]) + y_off[:, pp * 128:(pp + 1) * 128])
    y = jnp.concatenate(y_parts, axis=1) + d_ref[...] * xs
    y_ref[0] = _rms(y * _silu(zg_ref[0]), nrm_ref[...]).astype(y_ref.dtype)

    @pl.when(c == pl.num_programs(1) - 1)
    def _():
        for g in range(M2_GROUPS):
            st_ref[0, g * 256:(g + 1) * 256, :] = st_s[g].T
        cv_ref[0] = xbc_ref[0, q - (M2_CONV - 1):q, :]


def _ssd_prompt(xbc, dt_raw, zg, cw, cb, dtb, a, d, nrm, l):
    b, sl, _ = xbc.shape
    q = _tile(sl, M2_Q)
    tok = lambda n: pl.BlockSpec((1, q, n), lambda i, j: (i, j, 0))
    return pl.pallas_call(
        _ssd_prompt_kernel,
        grid=(b, sl // q),
        in_specs=[tok(M2_CONV_DIM), tok(M2_INNER), tok(M2_INNER),
                  _layer((M2_CONV, M2_CONV_DIM), l), _layer((1, M2_CONV_DIM), l),
                  _layer((1, M2_INNER), l), _layer((1, M2_INNER), l), _layer((1, M2_INNER), l),
                  _layer((1, M2_INNER), l)],
        out_specs=[tok(M2_INNER),
                   pl.BlockSpec((1, M2_INNER, M2_STATE), lambda i, j: (i, 0, 0)),
                   pl.BlockSpec((1, M2_CONV - 1, M2_CONV_DIM), lambda i, j: (i, 0, 0))],
        out_shape=[jax.ShapeDtypeStruct((b, sl, M2_INNER), BF16),
                   jax.ShapeDtypeStruct((b, M2_INNER, M2_STATE), F32),
                   jax.ShapeDtypeStruct((b, M2_CONV - 1, M2_CONV_DIM), F32)],
        scratch_shapes=[pltpu.VMEM((8, M2_CONV_DIM), F32), pltpu.VMEM((M2_GROUPS, M2_STATE, 256), F32)],
        compiler_params=_cp("parallel", "arbitrary"),
        name="ssd_prompt",
    )(xbc, dt_raw, zg, cw, cb, dtb, a, d, nrm)


def _ssd_step_kernel(xbc_ref, cs_ref, dt_ref, zg_ref, st_ref, cw_ref, cb_ref, dtb_ref, a_ref, d_ref, nrm_ref,
                     y_ref, sto_ref, cvo_ref):
    nb = xbc_ref.shape[0]
    x = xbc_ref[...]
    b0, b1, b2 = cs_ref[:, 0, :], cs_ref[:, 1, :], cs_ref[:, 2, :]
    conv = (cb_ref[...] + cw_ref[0:1, :] * b0 + cw_ref[1:2, :] * b1 + cw_ref[2:3, :] * b2 + cw_ref[3:4, :] * x)
    cvo_ref[:, 0, :] = b1
    cvo_ref[:, 1, :] = b2
    cvo_ref[:, 2, :] = x
    xc = _silu(conv)
    xs = xc[:, 0:M2_INNER]
    dt = _softplus(dt_ref[...] + dtb_ref[...])
    dec = jnp.exp(dt * a_ref[...])
    pad = jnp.zeros((128 - nb, M2_INNER), F32)
    xdt_t = jnp.concatenate([xs * dt, pad], axis=0).T
    dec_t = jnp.concatenate([dec, pad], axis=0).T
    hrow = lax.broadcasted_iota(jnp.int32, (M2_INNER, 1), 0)
    lane = lax.broadcasted_iota(jnp.int32, (1, 128), 1)
    ycols = jnp.zeros((M2_INNER, 128), F32)
    for b in range(nb):
        brow = jnp.where(hrow < 256, xc[b:b + 1, 512:640], xc[b:b + 1, 640:768])
        crow = jnp.where(hrow < 256, xc[b:b + 1, 768:896], xc[b:b + 1, 896:1024])
        h_new = dec_t[:, b:b + 1] * st_ref[b] + xdt_t[:, b:b + 1] * brow
        sto_ref[b] = h_new
        ycols = jnp.where(lane == b, jnp.sum(h_new * crow, axis=1, keepdims=True), ycols)
    y = ycols.T[0:nb] + d_ref[...] * xs
    y_ref[...] = _rms(y * _silu(zg_ref[...]), nrm_ref[...])


def _ssd_step(xbc, conv_state, dt_raw, zg, ssm_state, cw, cb, dtb, a, d, nrm, l):
    m = xbc.shape[0]
    nb = 8
    tok = lambda n: pl.BlockSpec((nb, n), lambda i: (i, 0))
    depth = ssm_state.shape[0]
    st = ssm_state.reshape(depth, m, M2_INNER, M2_STATE)
    return pl.pallas_call(
        _ssd_step_kernel,
        grid=(m // nb,),
        in_specs=[tok(M2_CONV_DIM),
                  pl.BlockSpec((None, nb, M2_CONV - 1, M2_CONV_DIM), lambda i: (l, i, 0, 0)),
                  tok(M2_INNER), tok(M2_INNER),
                  pl.BlockSpec((None, nb, M2_INNER, M2_STATE), lambda i: (l, i, 0, 0)),
                  _layer((M2_CONV, M2_CONV_DIM), l), _layer((1, M2_CONV_DIM), l),
                  _layer((1, M2_INNER), l), _layer((1, M2_INNER), l), _layer((1, M2_INNER), l),
                  _layer((1, M2_INNER), l)],
        out_specs=[tok(M2_INNER),
                   pl.BlockSpec((nb, M2_INNER, M2_STATE), lambda i: (i, 0, 0)),
                   pl.BlockSpec((nb, M2_CONV - 1, M2_CONV_DIM), lambda i: (i, 0, 0))],
        out_shape=[jax.ShapeDtypeStruct((m, M2_INNER), F32),
                   jax.ShapeDtypeStruct((m, M2_INNER, M2_STATE), F32),
                   jax.ShapeDtypeStruct((m, M2_CONV - 1, M2_CONV_DIM), F32)],
        compiler_params=_cp("parallel"),
        name="ssd_step",
    )(xbc, conv_state, dt_raw, zg, st, cw, cb, dtb, a, d, nrm)


def _merge_kernel(split5, h_ref, y5_ref, ysb_ref, ym_ref, gate_ref, w5_ref, wsb_ref, wm_ref, wo_ref, o_ref):
    tm = h_ref.shape[0]

    def branch(y, w_ref, lo):
        return _sigmoid(gate_ref[:, lo:lo + D_MODEL].astype(F32)) * jnp.dot(
            y.astype(BF16), w_ref[...], preferred_element_type=F32)

    if split5:
        y5 = jnp.concatenate([y5_ref[pl.ds(j, tm, stride=4), :] for j in range(4)], axis=1)
    else:
        y5 = y5_ref[...]
    merged = (branch(y5, w5_ref, 0) + branch(ysb_ref[...], wsb_ref, D_MODEL)
              + branch(ym_ref[...], wm_ref, 2 * D_MODEL))
    o_ref[...] = h_ref[...] + jnp.dot(merged.astype(BF16), wo_ref[...], preferred_element_type=F32)


def _merge(h, y5, ysb, ym, gates, w5, wsb, wm, wo, l):
    m = h.shape[0]
    tm = _tile(m, 512)
    row = lambda n: pl.BlockSpec((tm, n), lambda i: (i, 0))
    split5 = y5.shape[-1] == 128
    y5_spec = pl.BlockSpec((tm * 4, 128), lambda i: (i, 0)) if split5 else row(512)
    return pl.pallas_call(
        functools.partial(_merge_kernel, split5),
        grid=(m // tm,),
        in_specs=[row(D_MODEL), y5_spec, row(512), row(512), row(3 * D_MODEL),
                  _layer((512, D_MODEL), l), _layer((512, D_MODEL), l), _layer((512, D_MODEL), l),
                  _layer((D_MODEL, D_MODEL), l)],
        out_specs=row(D_MODEL),
        out_shape=jax.ShapeDtypeStruct((m, D_MODEL), F32),
        compiler_params=_cp("parallel"),
        name="merge",
    )(h, y5, ysb, ym, gates, w5, wsb, wm, wo)


def _mem_kv_kernel(x_ref, wk_ref, wv_ref, k_ref, v_ref, kb_ref, vb_ref):
    xb = x_ref[...].astype(BF16)
    k = jnp.dot(xb, wk_ref[...], preferred_element_type=F32)
    v = jnp.dot(xb, wv_ref[...], preferred_element_type=F32)
    k_ref[...] = k
    v_ref[...] = v
    kb_ref[...] = k.astype(BF16)
    vb_ref[...] = v.astype(BF16)


def _mem_kv(mem, wk, wv, l):
    m = mem.shape[0]
    tm = _tile(m, 512)
    row = pl.BlockSpec((tm, D_MODEL), lambda i: (i, 0))
    return pl.pallas_call(
        _mem_kv_kernel,
        grid=(m // tm,),
        in_specs=[row, _layer((D_MODEL, D_MODEL), l), _layer((D_MODEL, D_MODEL), l)],
        out_specs=[row] * 4,
        out_shape=[jax.ShapeDtypeStruct((m, D_MODEL), F32)] * 2 + [jax.ShapeDtypeStruct((m, D_MODEL), BF16)] * 2,
        compiler_params=_cp("parallel"),
        name="mem_kv",
    )(mem, wk, wv)


def _xattn_prompt_kernel(h_ref, g_ref, wq_ref, mk_ref, mv_ref, wo_ref, o_ref):
    h = h_ref[0]
    xn = _rms(h, g_ref[...]).astype(BF16)
    q = (jnp.dot(xn, wq_ref[...], preferred_element_type=F32) * (MEM_HEAD_DIM ** -0.5)).astype(BF16)
    heads = []
    for hd in range(MEM_HEADS):
        sl = slice(hd * MEM_HEAD_DIM, (hd + 1) * MEM_HEAD_DIM)
        s = _dot_nt(q[:, sl], mk_ref[0, :, sl])
        e = jnp.exp(s - jnp.max(s, axis=-1, keepdims=True))
        p = e / jnp.sum(e, axis=-1, keepdims=True)
        heads.append(jnp.dot(p.astype(BF16), mv_ref[0, :, sl], preferred_element_type=F32))
    o = jnp.concatenate(heads, axis=1).astype(BF16)
    o_ref[0] = h + jnp.dot(o, wo_ref[...], preferred_element_type=F32)


def _xattn_prompt(h, g, wq, mkb, mvb, wo, l):
    b, sl, _ = h.shape
    tq = _tile(sl, 512)
    mt = mkb.shape[1]
    return pl.pallas_call(
        _xattn_prompt_kernel,
        grid=(b, sl // tq),
        in_specs=[pl.BlockSpec((1, tq, D_MODEL), lambda i, j: (i, j, 0)),
                  _layer((1, D_MODEL), l), _layer((D_MODEL, D_MODEL), l),
                  pl.BlockSpec((1, mt, D_MODEL), lambda i, j: (i, 0, 0)),
                  pl.BlockSpec((1, mt, D_MODEL), lambda i, j: (i, 0, 0)),
                  _layer((D_MODEL, D_MODEL), l)],
        out_specs=pl.BlockSpec((1, tq, D_MODEL), lambda i, j: (i, j, 0)),
        out_shape=jax.ShapeDtypeStruct((b, sl, D_MODEL), F32),
        compiler_params=_cp("parallel", "parallel"),
        name="xattn_prompt",
    )(h, g, wq, mkb, mvb, wo)


def _xattn_sample_kernel(h_ref, g_ref, wq_ref, mk_ref, mv_ref, wo_ref, o_ref, q_s, att_s):
    b = pl.program_id(0)

    @pl.when(b == 0)
    def _():
        xn = _rms(h_ref[...], g_ref[...]).astype(BF16)
        q_s[...] = jnp.dot(xn, wq_ref[...], preferred_element_type=F32) * (MEM_HEAD_DIM ** -0.5)

    head_of_row = lax.broadcasted_iota(jnp.int32, (D_MODEL, 128), 0) // MEM_HEAD_DIM
    e_mat = (head_of_row == lax.broadcasted_iota(jnp.int32, (D_MODEL, 128), 1)).astype(BF16)
    head_of_col = lax.broadcasted_iota(jnp.int32, (128, D_MODEL), 1) // MEM_HEAD_DIM
    e_t = (head_of_col == lax.broadcasted_iota(jnp.int32, (128, D_MODEL), 0)).astype(BF16)
    q = q_s[pl.ds(b, 1), :]
    s = jnp.dot((mk_ref[...] * q).astype(BF16), e_mat, preferred_element_type=F32)
    e = jnp.exp(s - jnp.max(s, axis=0, keepdims=True))
    p = e / jnp.sum(e, axis=0, keepdims=True)
    pv = jnp.dot(p.astype(BF16), e_t, preferred_element_type=F32) * mv_ref[...]
    att_s[pl.ds(b, 1), :] = jnp.sum(pv, axis=0, keepdims=True)

    @pl.when(b == pl.num_programs(0) - 1)
    def _():
        o_ref[...] = h_ref[...] + jnp.dot(att_s[...].astype(BF16), wo_ref[...], preferred_element_type=F32)


def _xattn_sample(h, g, wq, cache_k, cache_v, wo, l):
    m = h.shape[0]
    depth, _, mt = cache_k.shape[:3]
    ck = cache_k.reshape(depth, m, mt, D_MODEL)
    cv = cache_v.reshape(depth, m, mt, D_MODEL)
    kv = pl.BlockSpec((None, None, mt, D_MODEL), lambda i: (l, i, 0, 0))
    return pl.pallas_call(
        _xattn_sample_kernel,
        grid=(m,),
        in_specs=[_full((m, D_MODEL)), _layer((1, D_MODEL), l), _layer((D_MODEL, D_MODEL), l), kv, kv,
                  _layer((D_MODEL, D_MODEL), l)],
        out_specs=_full((m, D_MODEL)),
        out_shape=jax.ShapeDtypeStruct((m, D_MODEL), F32),
        scratch_shapes=[pltpu.VMEM((m, D_MODEL), F32), pltpu.VMEM((m, D_MODEL), F32)],
        compiler_params=_cp("arbitrary"),
        name="xattn_sample",
    )(h, g, wq, ck, cv, wo)


def _mlp_kernel(h_ref, g_ref, w1_ref, w2_ref, o_ref):
    h = h_ref[...]
    xn = _rms(h, g_ref[...]).astype(BF16)
    acc = h
    for j in range(D_FF // 1024):
        a = jnp.maximum(jnp.dot(xn, w1_ref[:, j * 1024:(j + 1) * 1024], preferred_element_type=F32), 0.0)
        acc = acc + jnp.dot((a * a).astype(BF16), w2_ref[j * 1024:(j + 1) * 1024, :], preferred_element_type=F32)
    o_ref[...] = acc


def _mlp(h, g, w1, w2, l):
    m = h.shape[0]
    tm = _tile(m, 512)
    row = pl.BlockSpec((tm, D_MODEL), lambda i: (i, 0))
    return pl.pallas_call(
        _mlp_kernel,
        grid=(m // tm,),
        in_specs=[row, _layer((1, D_MODEL), l), _layer((D_MODEL, D_FF), l), _layer((D_FF, D_MODEL), l)],
        out_specs=row,
        out_shape=jax.ShapeDtypeStruct((m, D_MODEL), F32),
        compiler_params=_cp("parallel"),
        name="mlp",
    )(h, g, w1, w2)


def _norm_kernel(h_ref, g_ref, o_ref):
    o_ref[...] = _rms(h_ref[...], g_ref[...])


def _final_norm(h, g):
    m = h.shape[0]
    tm = _tile(m, 1024)
    row = pl.BlockSpec((tm, D_MODEL), lambda i: (i, 0))
    return pl.pallas_call(
        _norm_kernel,
        grid=(m // tm,),
        in_specs=[row, _full((1, D_MODEL))],
        out_specs=row,
        out_shape=jax.ShapeDtypeStruct((m, D_MODEL), F32),
        compiler_params=_cp("parallel"),
        name="final_norm",
    )(h, g)


def _block_diag(blocks):
    d, g, r, c = blocks.shape
    eye = jnp.eye(g, dtype=blocks.dtype)
    return (blocks[:, :, :, None, :] * eye[None, :, None, :, None]).reshape(d, g * r, g * c)


def kernel(x_prompt, x_sample, mem_prompt, cache_sb_k, cache_sb_v, state_s5_re, state_s5_im, state_ssm, state_conv, cache_mem_k, cache_mem_v, page_table, norm_mix, w_in, s5_lambda_re, s5_lambda_im, s5_log_dt, s5_b_re, s5_b_im, s5_c_re, s5_c_im, s5_d, s5_w_glu, s5_b_glu, sb_beta_bias, m2_conv_w, m2_conv_b, m2_dt_bias, m2_a_log, m2_d, m2_norm, w_br_s5, w_br_sb, w_br_m2, w_out, norm_mem, mem_wq, mem_wk, mem_wv, mem_wo, norm_mlp, mlp_w1, mlp_w2, norm_final):
    depth = w_in.shape[0]
    bsz, sl, _ = x_prompt.shape
    dec_b = x_sample.shape[0]
    mt = mem_prompt.shape[1]
    rep = lambda p: jnp.repeat(p, M2_HEAD_DIM, axis=-1)[:, None, :]
    row = lambda p: p[:, None, :]

    w_all = jnp.concatenate([w_in[:, :, :IN_MAIN], w_in[:, :, IN_GATE:],
                             jnp.repeat(w_in[:, :, IN_DT:IN_GATE], M2_HEAD_DIM, axis=-1)], axis=-1).astype(BF16)
    wglu, w5, wsb, wm, wo = (w.astype(BF16) for w in (s5_w_glu, w_br_s5, w_br_sb, w_br_m2, w_out))
    wq, wk, wv, wmo, w1, w2 = (w.astype(BF16) for w in (mem_wq, mem_wk, mem_wv, mem_wo, mlp_w1, mlp_w2))
    g_mix, g_mem, g_mlp = row(norm_mix), row(norm_mem), row(norm_mlp)
    s5d, bglu, cb = row(s5_d), row(s5_b_glu), row(m2_conv_b)
    dtb, a_neg, m2d, m2n = rep(m2_dt_bias), rep(-jnp.exp(m2_a_log.astype(F32))), rep(m2_d), row(m2_norm)
    tri_t = (jnp.arange(SB_TILE)[:, None] <= jnp.arange(SB_TILE)[None, :]).astype(BF16)
    bias_col = jnp.broadcast_to(sb_beta_bias[:, :, None], (depth, SB_HEADS, 128))
    h0r = state_s5_re.reshape(depth, dec_b, 2048)
    h0i = state_s5_im.reshape(depth, dec_b, 2048)
    mem2 = mem_prompt.reshape(bsz * mt, D_MODEL)

    hp = x_prompt.reshape(bsz * sl, D_MODEL)
    hs = x_sample.reshape(dec_b, D_MODEL)
    outs = [[] for _ in range(14)]
    flat = lambda p: p.reshape((depth * S5_GROUPS,) + p.shape[2:])
    wcat, poutr, pouti, pw, bbr, bbi = _s5_prep(flat(s5_lambda_re), flat(s5_lambda_im), flat(s5_log_dt),
                                                 flat(s5_b_re), flat(s5_b_im), flat(s5_c_re), flat(s5_c_im))
    power = lambda part, k: pw[:, part, k].reshape(depth, 1, 2048)
    a16r, a16i, a1r, a1i = power(0, S5_T), power(1, S5_T), power(0, 1), power(1, 1)
    per_layer = lambda p: p.reshape((depth, S5_GROUPS) + p.shape[1:])
    bbr_d = _block_diag(jnp.swapaxes(per_layer(bbr[:, :, :S5_CH]), 2, 3)).astype(BF16)
    bbi_d = _block_diag(jnp.swapaxes(per_layer(bbi[:, :, :S5_CH]), 2, 3)).astype(BF16)
    ccr_d = _block_diag(jnp.swapaxes(s5_c_re, 2, 3)).astype(BF16)
    cci_d = _block_diag(jnp.swapaxes(s5_c_im, 2, 3)).astype(BF16)

    for l in range(depth):

        u4, kt, vt, vtb, q2, k2, zg, xbc, gates, dtr = _in_proj_prompt(hp, g_mix, w_all, sb_beta_bias, l, bsz)
        r3 = lambda t: t.reshape(bsz, sl, t.shape[-1])
        y5, s5r, s5i = _s5_prompt(u4.reshape(bsz, sl * 4, 128), wcat, poutr, pouti, a16r, a16i,
                                  s5d, wglu, bglu, l)
        ysb = _sb_prompt(r3(q2), r3(k2), vtb, tri_t)
        ym, ssm, cvs = _ssd_prompt(r3(xbc), r3(dtr), r3(zg), m2_conv_w, cb, dtb, a_neg, m2d, m2n, l)
        hp = _merge(hp, y5.reshape(bsz * sl * 4, 128), ysb.reshape(bsz * sl, SB_WIDTH),
                    ym.reshape(bsz * sl, M2_INNER), gates, w5, wsb, wm, wo, l)
        mk, mv, mkb, mvb = _mem_kv(mem2, wk, wv, l)
        hp = _xattn_prompt(hp.reshape(bsz, sl, D_MODEL), g_mem, wq, mkb.reshape(bsz, mt, D_MODEL),
                           mvb.reshape(bsz, mt, D_MODEL), wmo, l).reshape(bsz * sl, D_MODEL)
        hp = _mlp(hp, g_mlp, w1, w2, l)
        for lst, val in zip(outs[:8], (
                kt, vt, s5r.reshape(bsz, S5_GROUPS, S5_STATE), s5i.reshape(bsz, S5_GROUPS, S5_STATE),
                ssm.reshape(bsz, M2_HEADS, M2_HEAD_DIM, M2_STATE), cvs,
                mk.reshape(bsz, mt, MEM_HEADS, MEM_HEAD_DIM), mv.reshape(bsz, mt, MEM_HEADS, MEM_HEAD_DIM))):
            lst.append(val)

        u, k, v, qs, zg, xbc, gates, dtr = _in_proj_sample(hs, g_mix, w_all, l)
        y5, s5r, s5i = _s5_step(u, h0r, h0i, a1r, a1i, bbr_d, bbi_d, ccr_d, cci_d, s5d, wglu, bglu, l)
        ysb = _sb_decode(qs, bias_col, cache_sb_k, cache_sb_v, page_table, l)
        ym, ssm, cvs = _ssd_step(xbc, state_conv, dtr, zg, state_ssm, m2_conv_w, cb, dtb, a_neg, m2d, m2n, l)
        hs = _merge(hs, y5, ysb, ym, gates, w5, wsb, wm, wo, l)
        hs = _xattn_sample(hs, g_mem, wq, cache_mem_k, cache_mem_v, wmo, l)
        hs = _mlp(hs, g_mlp, w1, w2, l)
        for lst, val in zip(outs[8:], (
                k.reshape(dec_b, 1, SB_HEADS, SB_HEAD_DIM), v.reshape(dec_b, 1, SB_HEADS, SB_HEAD_DIM),
                s5r.reshape(dec_b, S5_GROUPS, S5_STATE), s5i.reshape(dec_b, S5_GROUPS, S5_STATE),
                ssm.reshape(dec_b, M2_HEADS, M2_HEAD_DIM, M2_STATE), cvs)):
            lst.append(val)

    g_fin = norm_final[None, :]
    y_prompt = _final_norm(hp, g_fin).reshape(bsz, sl, D_MODEL)
    y_sample = _final_norm(hs, g_fin).reshape(dec_b, 1, D_MODEL)
    stacked = [jnp.stack(o) for o in outs]
    for i in (0, 1):
        stacked[i] = jnp.transpose(stacked[i].reshape(depth, bsz, SB_HEADS, SB_HEAD_DIM, sl), (0, 1, 4, 2, 3))
    return (y_prompt, y_sample) + tuple(stacked)
```

```python
import functools
import math

import jax
import jax.numpy as jnp
from jax import lax
from jax.experimental import pallas as pl
from jax.experimental.pallas import tpu as pltpu

F32 = jnp.float32
BF16 = jnp.bfloat16

D_MODEL = 1024
PAGE_SIZE = 128
S5_CH = 16
S5_WIDTH = 512
S5_GROUPS = 32
S5_STATE = 64
S5_LBLK = 2048
S5_T = 16
SB_WIDTH = 512
SB_HEADS = 8
SB_HEAD_DIM = 64
SB_TILE = 256
SB_QTILE = 1024
SB_UNROLL = 4
M2_INNER = 512
M2_HEADS = 8
M2_HEAD_DIM = 64
M2_GROUPS = 2
M2_STATE = 128
M2_CONV = 4
M2_CONV_DIM = 1024
M2_Q = 256
MEM_HEADS = 4
MEM_HEAD_DIM = 256
D_FF = 4096
EPS = 1e-6
LOG2E = 1.4426950408889634
SOFTPLUS_LINEAR_ABOVE = 30.0
IN_MAIN = 3584
IN_DT = 3584
IN_GATE = 3592
VMEM_LIMIT_V7X = 56 * 1024 * 1024


def _cp(*sem):
    return pltpu.CompilerParams(dimension_semantics=sem, vmem_limit_bytes=VMEM_LIMIT_V7X)


def _tile(m, pref):
    t = min(m, pref)
    while m % t:
        t -= 8
    return t


def _full(shape):
    nd = len(shape)
    return pl.BlockSpec(shape, lambda *_: (0,) * nd)


def _layer(shape, l):
    nd = len(shape)
    return pl.BlockSpec((None,) + tuple(shape), lambda *_: (l,) + (0,) * nd)


def _rms(x, g):
    return x * lax.rsqrt(jnp.mean(x * x, axis=-1, keepdims=True) + EPS) * g


def _bdot(a, b):
    return jnp.dot(a.astype(BF16), b.astype(BF16), preferred_element_type=F32)


def _dot_nt(a, b):
    return lax.dot_general(a, b, (((1,), (1,)), ((), ())), preferred_element_type=F32)


def _dot_tn(a, b):
    return lax.dot_general(a, b, (((0,), (0,)), ((), ())), preferred_element_type=F32)


def _softplus(z):
    return jnp.maximum(z, 0.0) + jnp.log(1.0 + jnp.exp(-jnp.abs(z)))


def _sigmoid(z):
    return 1.0 / (1.0 + jnp.exp(-z))


def _silu(z):
    return z * _sigmoid(z)


def _gelu_tanh(x):
    return 0.5 * x * (1.0 + jnp.tanh(math.sqrt(2.0 / math.pi) * (x + 0.044715 * (x * x * x))))


def _in_proj_common(x_ref, g_ref, w_ref, zg_ref, xbc_ref, gate_ref, dt_ref):
    xn = _rms(x_ref[...], g_ref[...]).astype(BF16)

    def mm(lo, hi):
        return jnp.dot(xn, w_ref[:, lo:hi], preferred_element_type=F32)

    zg_ref[...] = mm(2048, 2560)
    xbc_ref[...] = mm(2560, 3584)
    gate_ref[...] = mm(3584, 6656).astype(gate_ref.dtype)
    dt_ref[...] = mm(6656, 7168)
    return mm


def _in_proj_prompt_kernel(l, bias_ref, x_ref, g_ref, w_ref, u4_ref, kt_ref, vt_ref, vtb_ref, q2_ref, k2_ref,
                           zg_ref, xbc_ref, gate_ref, dt_ref):
    mm = _in_proj_common(x_ref, g_ref, w_ref, zg_ref, xbc_ref, gate_ref, dt_ref)
    tm = x_ref.shape[0]
    u = mm(0, 512)
    for j in range(4):
        u4_ref[pl.ds(j, tm, stride=4), :] = u[:, j * 128:(j + 1) * 128]
    q = mm(512, 1024) * (SB_HEAD_DIM ** -0.5)
    k = mm(1024, 1536)
    v = mm(1536, 2048)
    kt_ref[...] = k.T
    vt_ref[...] = v.T
    vtb_ref[...] = v.astype(BF16).T
    lane = lax.broadcasted_iota(jnp.int32, (tm, 128), 1)
    for h in range(SB_HEADS):
        sl = slice((h // 2) * 128, (h // 2 + 1) * 128)
        qh, kh = q[:, sl], k[:, sl]
        if h % 2:
            qh, kh = pltpu.roll(qh, 64, 1), pltpu.roll(kh, 64, 1)
        extra = lane == SB_HEAD_DIM
        q2_ref[:, h * 128:(h + 1) * 128] = jnp.where(
            lane < SB_HEAD_DIM, qh, jnp.where(extra, bias_ref[l, h], 0.0)).astype(BF16)
        k2_ref[:, h * 128:(h + 1) * 128] = jnp.where(
            lane < SB_HEAD_DIM, kh, jnp.where(extra, 1.0, 0.0)).astype(BF16)


def _in_proj_prompt(x, g, w_all, bias, l, bsz):
    m = x.shape[0]
    sl = m // bsz
    tm = _tile(sl, 256)
    per_b = sl // tm
    row = lambda n: pl.BlockSpec((tm, n), lambda i: (i, 0))
    tr = pl.BlockSpec((None, 512, tm), lambda i: (i // per_b, 0, i % per_b))
    outs = [(1024, BF16), (1024, BF16), (512, F32), (1024, F32), (3072, BF16), (512, F32)]
    return pl.pallas_call(
        functools.partial(_in_proj_prompt_kernel, l),
        grid=(m // tm,),
        in_specs=[pl.BlockSpec(memory_space=pltpu.SMEM),
                  row(D_MODEL), _layer((1, D_MODEL), l), _layer((D_MODEL, 7168), l)],
        out_specs=[pl.BlockSpec((tm * 4, 128), lambda i: (i, 0)), tr, tr, tr] + [row(n) for n, _ in outs],
        out_shape=[jax.ShapeDtypeStruct((m * 4, 128), F32),
                   jax.ShapeDtypeStruct((bsz, 512, sl), F32), jax.ShapeDtypeStruct((bsz, 512, sl), F32),
                   jax.ShapeDtypeStruct((bsz, 512, sl), BF16)]
                  + [jax.ShapeDtypeStruct((m, n), dt) for n, dt in outs],
        compiler_params=_cp("parallel"),
        name="in_proj",
    )(bias, x, g, w_all)


def _in_proj_sample_kernel(x_ref, g_ref, w_ref, u_ref, k_ref, v_ref, q_ref, zg_ref, xbc_ref, gate_ref, dt_ref):
    mm = _in_proj_common(x_ref, g_ref, w_ref, zg_ref, xbc_ref, gate_ref, dt_ref)
    u_ref[...] = mm(0, 512)
    q_ref[...] = mm(512, 1024) * (SB_HEAD_DIM ** -0.5)
    k_ref[...] = mm(1024, 1536)
    v_ref[...] = mm(1536, 2048)


def _in_proj_sample(x, g, w_all, l):
    m = x.shape[0]
    outs = [512, 512, 512, 512, 512, 1024, 3072, 512]
    return pl.pallas_call(
        _in_proj_sample_kernel,
        grid=(1,),
        in_specs=[_full((m, D_MODEL)), _layer((1, D_MODEL), l), _layer((D_MODEL, 7168), l)],
        out_specs=[_full((m, n)) for n in outs],
        out_shape=[jax.ShapeDtypeStruct((m, n), F32) for n in outs],
        compiler_params=_cp("arbitrary"),
        name="in_proj_step",
    )(x, g, w_all)


def _s5_prep_kernel(lr_row, li_row, lr_col, li_col, ldt_ref, btr_ref, bti_ref, ctr_ref, cti_ref,
                    wcat_ref, poutr_ref, pouti_ref, pw_ref, bbr_ref, bbi_ref):
    dt = jnp.exp(ldt_ref[0])
    lr, li = lr_row[0], li_row[0]
    kk = lax.broadcasted_iota(jnp.int32, (24, 1), 0).astype(F32)
    mag = jnp.exp(kk * (lr * dt))
    ang = kk * (li * dt)
    p_re, p_im = mag * jnp.cos(ang), mag * jnp.sin(ang)
    pw_ref[0, 0] = p_re
    pw_ref[0, 1] = p_im

    def rep(p, lo):
        return jnp.broadcast_to(p[lo:lo + S5_T][:, None, :], (S5_T, S5_CH, S5_STATE)).reshape(256, S5_STATE)

    c_re, c_im = ctr_ref[0], cti_ref[0]
    pr0, pi0 = rep(p_re, 0), rep(p_im, 0)
    l_re = c_re * pr0 - c_im * pi0
    l_im = c_re * pi0 + c_im * pr0
    pr1, pi1 = rep(p_re, 1), rep(p_im, 1)
    poutr_ref[0] = (c_re * pr1 - c_im * pi1).astype(BF16)
    pouti_ref[0] = (-(c_re * pi1 + c_im * pr1)).astype(BF16)

    lrc, lic = lr_col[0], li_col[0]
    m1 = jnp.exp(lrc * dt)
    a_re, a_im = m1 * jnp.cos(lic * dt), m1 * jnp.sin(lic * dt)
    den = lrc * lrc + lic * lic
    f_re = ((a_re - 1.0) * lrc + a_im * lic) / den
    f_im = (a_im * lrc - (a_re - 1.0) * lic) / den
    b_re, b_im = btr_ref[0], bti_ref[0]
    bb_re = f_re * b_re - f_im * b_im
    bb_im = f_re * b_im + f_im * b_re
    bbr_ref[0] = bb_re
    bbi_ref[0] = bb_im

    hp = lax.Precision.HIGHEST
    kt = (jnp.dot(l_re, bb_re, precision=hp, preferred_element_type=F32)
          - jnp.dot(l_im, bb_im, precision=hp, preferred_element_type=F32))
    lane_s = lax.broadcasted_iota(jnp.int32, (1, 256), 1) // S5_CH
    toep = jnp.where(lane_s == 0, kt, 0.0)
    for s in range(1, S5_T):
        sh = jnp.concatenate([jnp.zeros((S5_CH * s, 256), F32), kt[:256 - S5_CH * s]], axis=0)
        toep = jnp.where(lane_s == s, sh, toep)

    ks = (S5_T - 1 - lane_s).astype(F32)
    magc = jnp.exp(ks * (lrc * dt))
    angc = ks * (lic * dt)
    q_re, q_im = magc * jnp.cos(angc), magc * jnp.sin(angc)
    wcat_ref[0, 0:256] = toep.astype(BF16)
    wcat_ref[0, 256:320] = (q_re * bb_re - q_im * bb_im).astype(BF16)
    wcat_ref[0, 320:384] = (q_re * bb_im + q_im * bb_re).astype(BF16)


def _s5_prep(lam_re, lam_im, log_dt, b_re, b_im, c_re, c_im):
    g = lam_re.shape[0]
    per = lambda *shape: pl.BlockSpec((1,) + shape, lambda i: (i,) + (0,) * len(shape))
    tile_b = lambda b: jnp.tile(b, (1, 1, S5_T))
    tile_c = lambda c: jnp.tile(c, (1, S5_T, 1))
    return pl.pallas_call(
        _s5_prep_kernel,
        grid=(g,),
        in_specs=[per(1, 64), per(1, 64), per(64, 1), per(64, 1), per(1, 1),
                  per(64, 256), per(64, 256), per(256, 64), per(256, 64)],
        out_specs=[per(384, 256), per(256, 64), per(256, 64), per(2, 24, 64), per(64, 256), per(64, 256)],
        out_shape=[jax.ShapeDtypeStruct((g, 384, 256), BF16),
                   jax.ShapeDtypeStruct((g, 256, 64), BF16),
                   jax.ShapeDtypeStruct((g, 256, 64), BF16),
                   jax.ShapeDtypeStruct((g, 2, 24, 64), F32),
                   jax.ShapeDtypeStruct((g, 64, 256), F32),
                   jax.ShapeDtypeStruct((g, 64, 256), F32)],
        compiler_params=_cp("parallel"),
        name="s5_prep",
    )(lam_re[:, None, :], lam_im[:, None, :], lam_re[:, :, None], lam_im[:, :, None],
      log_dt[:, None, None], tile_b(b_re), tile_b(b_im), tile_c(c_re), tile_c(c_im))


def _rows(u_ref, s, j, n):
    return u_ref[0, pl.ds(s * 4 + j, n, stride=4 * S5_T), :]


def _s5_prompt_kernel(u_ref, wcat_ref, poutr_ref, pouti_ref, ar_ref, ai_ref, d_ref, wglu_ref, bglu_ref,
                      y_ref, sr_ref, si_ref, v_s, yt_s, hr_s, hi_s, cr_s, ci_s):
    n = v_s.shape[2]
    lb = pl.program_id(1)

    @pl.when(lb == 0)
    def _():
        cr_s[...] = jnp.zeros_like(cr_s)
        ci_s[...] = jnp.zeros_like(ci_s)

    for s in range(S5_T):
        for j in range(4):
            v_s[s, j * 128:(j + 1) * 128, :] = _rows(u_ref, s, j, n).T.astype(BF16)

    for gp in range(S5_GROUPS // 2):
        res = []
        for g in (2 * gp, 2 * gp + 1):
            vg = v_s[:, g * S5_CH:(g + 1) * S5_CH, :].reshape(S5_T * S5_CH, n)
            r = jnp.dot(wcat_ref[g], vg, preferred_element_type=F32)
            yt_s[g] = r[0:256]
            res.append(r)
        hr_s[:, gp * 128:(gp + 1) * 128] = jnp.concatenate([res[0][256:320], res[1][256:320]], axis=0).T
        hi_s[:, gp * 128:(gp + 1) * 128] = jnp.concatenate([res[0][320:384], res[1][320:384]], axis=0).T

    a_re, a_im = ar_ref[...], ai_ref[...]

    def step(i, carry):
        h_re, h_im = carry
        s_re = hr_s[pl.ds(i, 1), :]
        s_im = hi_s[pl.ds(i, 1), :]
        hr_s[pl.ds(i, 1), :] = h_re
        hi_s[pl.ds(i, 1), :] = h_im
        return (a_re * h_re - a_im * h_im + s_re, a_re * h_im + a_im * h_re + s_im)

    h_re, h_im = lax.fori_loop(0, n, step, (cr_s[...], ci_s[...]))
    cr_s[...] = h_re
    ci_s[...] = h_im
    sr_ref[0] = h_re
    si_ref[0] = h_im

    for gp in range(S5_GROUPS // 2):
        ht_re = hr_s[:, gp * 128:(gp + 1) * 128].T.astype(BF16)
        ht_im = hi_s[:, gp * 128:(gp + 1) * 128].T.astype(BF16)
        for j, g in enumerate((2 * gp, 2 * gp + 1)):
            yt_s[g] += (jnp.dot(poutr_ref[g], ht_re[64 * j:64 * j + 64], preferred_element_type=F32)
                        + jnp.dot(pouti_ref[g], ht_im[64 * j:64 * j + 64], preferred_element_type=F32))

    for t in range(S5_T):
        yt = yt_s[:, t * S5_CH:(t + 1) * S5_CH, :].reshape(S5_WIDTH, n).T
        y = yt + d_ref[...] * jnp.concatenate([_rows(u_ref, t, j, n) for j in range(4)], axis=1)
        gl = _gelu_tanh(y)
        o = gl * _sigmoid(jnp.dot(gl.astype(BF16), wglu_ref[...], preferred_element_type=F32) + bglu_ref[...])
        for j in range(4):
            y_ref[0, pl.ds(t * 4 + j, n, stride=4 * S5_T), :] = o[:, j * 128:(j + 1) * 128]


def _s5_prompt(u4, wcat, poutr, pouti, a16r, a16i, d, wglu, bglu, l):
    b, sl = u4.shape[0], u4.shape[1] // 4
    lblk = _tile(sl, S5_LBLK)
    n = lblk // S5_T
    grp = lambda r, c: pl.BlockSpec((S5_GROUPS, r, c), lambda i, j: (l, 0, 0))
    return pl.pallas_call(
        _s5_prompt_kernel,
        grid=(b, sl // lblk),
        in_specs=[pl.BlockSpec((1, lblk * 4, 128), lambda i, j: (i, j, 0)),
                  grp(384, 256), grp(256, 64), grp(256, 64),
                  _layer((1, 2048), l), _layer((1, 2048), l),
                  _layer((1, S5_WIDTH), l), _layer((S5_WIDTH, S5_WIDTH), l), _layer((1, S5_WIDTH), l)],
        out_specs=[pl.BlockSpec((1, lblk * 4, 128), lambda i, j: (i, j, 0)),
                   pl.BlockSpec((1, 1, 2048), lambda i, j: (i, 0, 0)),
                   pl.BlockSpec((1, 1, 2048), lambda i, j: (i, 0, 0))],
        out_shape=[jax.ShapeDtypeStruct((b, sl * 4, 128), F32),
                   jax.ShapeDtypeStruct((b, 1, 2048), F32),
                   jax.ShapeDtypeStruct((b, 1, 2048), F32)],
        scratch_shapes=[pltpu.VMEM((S5_T, S5_WIDTH, n), BF16),
                        pltpu.VMEM((S5_GROUPS, 256, n), F32),
                        pltpu.VMEM((n, 2048), F32), pltpu.VMEM((n, 2048), F32),
                        pltpu.VMEM((1, 2048), F32), pltpu.VMEM((1, 2048), F32)],
        compiler_params=_cp("parallel", "arbitrary"),
        name="s5_prompt",
    )(u4, wcat, poutr, pouti, a16r, a16i, d, wglu, bglu)


def _s5_step_kernel(u_ref, h0r_ref, h0i_ref, ar_ref, ai_ref, bbr_ref, bbi_ref, ccr_ref, cci_ref,
                    d_ref, wglu_ref, bglu_ref, y_ref, hr_ref, hi_ref):
    u = u_ref[...]
    ub = u.astype(BF16)
    a_re, a_im = ar_ref[...], ai_ref[...]
    h0r, h0i = h0r_ref[...], h0i_ref[...]
    h_re = a_re * h0r - a_im * h0i + jnp.dot(ub, bbr_ref[...], preferred_element_type=F32)
    h_im = a_re * h0i + a_im * h0r + jnp.dot(ub, bbi_ref[...], preferred_element_type=F32)
    hr_ref[...] = h_re
    hi_ref[...] = h_im
    y = (jnp.dot(h_re.astype(BF16), ccr_ref[...], preferred_element_type=F32)
         - jnp.dot(h_im.astype(BF16), cci_ref[...], preferred_element_type=F32) + d_ref[...] * u)
    gl = _gelu_tanh(y)
    y_ref[...] = gl * _sigmoid(jnp.dot(gl.astype(BF16), wglu_ref[...], preferred_element_type=F32) + bglu_ref[...])


def _s5_step(u, h0r, h0i, a1r, a1i, bbr, bbi, ccr, cci, d, wglu, bglu, l):
    m = u.shape[0]
    return pl.pallas_call(
        _s5_step_kernel,
        grid=(1,),
        in_specs=[_full((m, S5_WIDTH)), _layer((m, 2048), l), _layer((m, 2048), l),
                  _layer((1, 2048), l), _layer((1, 2048), l),
                  _layer((S5_WIDTH, 2048), l), _layer((S5_WIDTH, 2048), l),
                  _layer((2048, S5_WIDTH), l), _layer((2048, S5_WIDTH), l),
                  _layer((1, S5_WIDTH), l), _layer((S5_WIDTH, S5_WIDTH), l), _layer((1, S5_WIDTH), l)],
        out_specs=[_full((m, S5_WIDTH)), _full((m, 2048)), _full((m, 2048))],
        out_shape=[jax.ShapeDtypeStruct((m, S5_WIDTH), F32),
                   jax.ShapeDtypeStruct((m, 2048), F32), jax.ShapeDtypeStruct((m, 2048), F32)],
        compiler_params=_cp("arbitrary"),
        name="s5_step",
    )(u, h0r, h0i, a1r, a1i, bbr, bbi, ccr, cci, d, wglu, bglu)


def _sb_prompt_kernel(q_ref, k_ref, vt_ref, tri_ref, o_ref, acc_s, c_s):
    tq, tk, tc = q_ref.shape[1], SB_TILE, 256
    per = tq // tk
    qi = pl.program_id(2)
    chunks = [(hh, a) for hh in range(2) for a in range(tq // tc)]
    qs = [q_ref[0, a * tc:(a + 1) * tc, hh * 128:(hh + 1) * 128] for hh, a in chunks]
    key = lax.broadcasted_iota(jnp.int32, (tk, tc), 0)
    qry = lax.broadcasted_iota(jnp.int32, (tk, tc), 1)
    acc_s[...] = jnp.zeros_like(acc_s)
    c_s[...] = jnp.zeros_like(c_s)

    def block(kb, off):
        start = pl.multiple_of(kb * tk, tk)
        kblk = k_ref[0, pl.ds(start, tk), :]
        vt = vt_ref[0, :, pl.ds(start, tk)]
        live, masks = [], {}
        for i, (hh, a) in enumerate(chunks):
            if off is not None and off >= (a + 1) * tc - 1:
                continue
            live.append(i)
            if off is not None and off + tk - 1 >= a * tc:
                masks[i] = key + (off - a * tc) < qry
        def logits(i):
            hh = chunks[i][0]
            return _dot_nt(kblk[:, hh * 128:(hh + 1) * 128], qs[i])

        zs, rs = {live[0]: logits(live[0])}, {}
        for n_done, i in enumerate(live):
            if n_done + 1 < len(live):
                zs[live[n_done + 1]] = logits(live[n_done + 1])
            z = zs[i]
            sp = jnp.where(z > SOFTPLUS_LINEAR_ABOVE, z, jnp.log(1.0 + jnp.exp2(z * LOG2E)))
            if i in masks:
                sp = jnp.where(masks[i], sp, 0.0)
            rs[i] = jnp.dot(tri_ref[...], sp.astype(BF16), preferred_element_type=F32)
        for i in live:
            w = jnp.exp(zs[i] - rs[i])
            if i in masks:
                w = jnp.where(masks[i], w, 0.0)
            c = c_s[i]
            acc_s[i] += jnp.exp(-c) * jnp.dot(vt, w.astype(BF16), preferred_element_type=F32)
            c_s[i] = c + rs[i][0:1, :]

    for d in range(per - 1, -1, -1):
        block(qi * per + d, d * tk)
    n = qi * per
    trips = n // SB_UNROLL

    @pl.loop(0, trips)
    def _(j):
        for d in range(SB_UNROLL):
            block(n - 1 - j * SB_UNROLL - d, None)

    assert per % SB_UNROLL == 0 or SB_UNROLL == 2 * per
    if per % SB_UNROLL:
        @pl.when(n - trips * SB_UNROLL > 0)
        def _():
            for d in range(per - 1, -1, -1):
                block(d, None)

    chan = lax.broadcasted_iota(jnp.int32, (128, tc), 0)
    for a in range(tq // tc):
        o_ref[0, a * tc:(a + 1) * tc, :] = jnp.where(
            chan < SB_HEAD_DIM, acc_s[a], acc_s[tq // tc + a]).T.astype(o_ref.dtype)


def _sb_prompt(q2, k2, vtb, tri_t):
    b, _, sl = vtb.shape
    tq = _tile(sl, SB_QTILE)
    return pl.pallas_call(
        _sb_prompt_kernel,
        grid=(b, SB_HEADS // 2, sl // tq),
        in_specs=[pl.BlockSpec((1, tq, 256), lambda i, p, j: (i, j, p)),
                  pl.BlockSpec((1, sl, 256), lambda i, p, j: (i, 0, p)),
                  pl.BlockSpec((1, 128, sl), lambda i, p, j: (i, p, 0)),
                  _full((SB_TILE, SB_TILE))],
        out_specs=pl.BlockSpec((1, tq, 128), lambda i, p, j: (i, j, p)),
        out_shape=jax.ShapeDtypeStruct((b, sl, SB_WIDTH), BF16),
        scratch_shapes=[pltpu.VMEM((2 * tq // 256, 128, 256), F32), pltpu.VMEM((2 * tq // 256, 1, 256), F32)],
        compiler_params=_cp("parallel", "parallel", "arbitrary"),
        name="sb_prompt",
    )(q2, k2, vtb, tri_t)


def _split_bf16(x):
    hi = x.astype(BF16)
    return hi, (x - hi.astype(F32)).astype(BF16)


def _sb_decode_kernel(npg, pt_ref, q_ref, bias_ref, tri_ref, later_ref, *refs):
    k_refs, v_refs = refs[:npg], refs[npg:2 * npg]
    o_ref, qc_s, acc_s, c_s = refs[2 * npg:]
    j = pl.program_id(1)

    @pl.when(j == 0)
    def _():
        qc_s[...] = jnp.broadcast_to(q_ref[0], (128, SB_WIDTH)).T
        acc_s[...] = jnp.zeros_like(acc_s)
        c_s[...] = jnp.zeros_like(c_s)

    qc = qc_s[...]
    bias = jnp.concatenate([bias_ref[...]] * npg, axis=0)
    z = jnp.concatenate(
        [(k_refs[i][...] * qc).reshape(SB_HEADS, SB_HEAD_DIM, PAGE_SIZE).sum(axis=1) for i in range(npg)],
        axis=0) + bias
    lk = -_softplus(z)
    hi, lo = _split_bf16(lk)
    r_loc = jnp.dot(jnp.concatenate([hi, lo], axis=1), tri_ref[...], preferred_element_type=F32)
    tot = jnp.broadcast_to(r_loc[:, 0:1], r_loc.shape)
    hi, lo = _split_bf16(tot)
    c_in = c_s[...]
    r = (r_loc + jnp.dot(later_ref[...], jnp.concatenate([hi, lo], axis=0), preferred_element_type=F32)
         + jnp.concatenate([c_in] * npg, axis=0))
    w = jnp.exp(z + r)
    c_s[...] = r[0:SB_HEADS] - r_loc[0:SB_HEADS] + tot[0:SB_HEADS]
    acc = acc_s[...]
    for i in range(npg):
        w_rows = jnp.broadcast_to(w[i * SB_HEADS:(i + 1) * SB_HEADS][:, None, :],
                                  (SB_HEADS, SB_HEAD_DIM, PAGE_SIZE)).reshape(SB_WIDTH, PAGE_SIZE)
        acc = acc + v_refs[i][...] * w_rows
    acc_s[...] = acc

    @pl.when(j == pl.num_programs(1) - 1)
    def _():
        o_ref[0] = jnp.sum(acc.T, axis=0, keepdims=True)


def _sb_decode(q, bias_col, cache_k, cache_v, page_table, l):
    m = q.shape[0]
    n_pages = page_table.shape[1]
    npg = 16
    while n_pages % npg:
        npg //= 2
    steps = n_pages // npg
    rows = npg * SB_HEADS

    def page(i):
        return pl.BlockSpec((None, None, SB_WIDTH, PAGE_SIZE),
                            lambda b, j, pt: (l, pt[b, (steps - 1 - j) * npg + i], 0, 0))

    const = lambda shape: pl.BlockSpec(shape, lambda b, j, pt: (0,) * len(shape))
    grid_spec = pltpu.PrefetchScalarGridSpec(
        num_scalar_prefetch=1,
        grid=(m, steps),
        in_specs=[pl.BlockSpec((1, 1, SB_WIDTH), lambda b, j, pt: (b, 0, 0)),
                  pl.BlockSpec((None, SB_HEADS, 128), lambda b, j, pt: (l, 0, 0)),
                  const((2 * PAGE_SIZE, PAGE_SIZE)), const((rows, 2 * rows))]
                 + [page(i) for i in range(npg)] * 2,
        out_specs=pl.BlockSpec((1, 1, SB_WIDTH), lambda b, j, pt: (b, 0, 0)),
        scratch_shapes=[pltpu.VMEM((SB_WIDTH, 128), F32), pltpu.VMEM((SB_WIDTH, PAGE_SIZE), F32),
                        pltpu.VMEM((SB_HEADS, 128), F32)],
    )
    depth, n_pool = cache_k.shape[:2]
    ck = jnp.transpose(cache_k, (0, 1, 3, 4, 2)).reshape(depth, n_pool, SB_WIDTH, PAGE_SIZE)
    cv = jnp.transpose(cache_v, (0, 1, 3, 4, 2)).reshape(depth, n_pool, SB_WIDTH, PAGE_SIZE)
    tri = (jnp.arange(2 * PAGE_SIZE)[:, None] % PAGE_SIZE >= jnp.arange(PAGE_SIZE)[None, :]).astype(BF16)
    ridx = jnp.arange(rows)
    later = ((ridx[None, :] % SB_HEADS == ridx[:, None] % SB_HEADS)
             & (ridx[None, :] // SB_HEADS > ridx[:, None] // SB_HEADS))
    later = jnp.concatenate([later, later], axis=1).astype(BF16)
    out = pl.pallas_call(
        functools.partial(_sb_decode_kernel, npg),
        grid_spec=grid_spec,
        out_shape=jax.ShapeDtypeStruct((m, 1, SB_WIDTH), F32),
        compiler_params=_cp("parallel", "arbitrary"),
        name="sb_decode",
    )(page_table, q.reshape(m, 1, SB_WIDTH), bias_col, tri, later, *([ck] * npg), *([cv] * npg))
    return out.reshape(m, SB_WIDTH)


def _cumsum_rows(x):
    n = x.shape[0]
    row = lax.broadcasted_iota(jnp.int32, (n, 1), 0)
    k = 1
    while k < n:
        x = x + jnp.where(row >= k, pltpu.roll(x, k, 0), 0.0)
        k *= 2
    return x


def _ssd_prompt_kernel(xbc_ref, dt_ref, zg_ref, cw_ref, cb_ref, dtb_ref, a_ref, d_ref, nrm_ref,
                       y_ref, st_ref, cv_ref, prev_s, st_s):
    q = xbc_ref.shape[1]
    c = pl.program_id(1)

    @pl.when(c == 0)
    def _():
        prev_s[...] = jnp.zeros_like(prev_s)
        st_s[...] = jnp.zeros_like(st_s)

    u = xbc_ref[0]
    prev = prev_s[...]
    row8 = lax.broadcasted_iota(jnp.int32, (8, 1), 0)
    conv = cb_ref[...] + cw_ref[M2_CONV - 1:M2_CONV, :] * u
    for k in range(1, M2_CONV):
        ru = pltpu.roll(u, k, 0)
        top = jnp.where(row8 < k, pltpu.roll(prev, k, 0), ru[0:8])
        conv = conv + cw_ref[M2_CONV - 1 - k:M2_CONV - k, :] * jnp.concatenate([top, ru[8:]], axis=0)
    prev_s[...] = u[q - 8:q]
    xc = _silu(conv)
    xs = xc[:, 0:M2_INNER]
    dt = _softplus(dt_ref[0] + dtb_ref[...])
    acum = _cumsum_rows(dt * a_ref[...])
    a_last = acum[q - 1:q, :]
    xdt = xs * dt
    xw = (xdt * jnp.exp(a_last - acum)).astype(BF16)
    xdt_b = xdt.astype(BF16)
    e_acum = jnp.exp(acum)
    rows = lax.broadcasted_iota(jnp.int32, (q, q), 0)
    cols = lax.broadcasted_iota(jnp.int32, (q, q), 1)
    causal = cols <= rows
    lane = lax.broadcasted_iota(jnp.int32, (q, 128), 1)
    y_parts = []
    for g in range(M2_GROUPS):
        bm = xc[:, M2_INNER + g * M2_STATE:M2_INNER + (g + 1) * M2_STATE].astype(BF16)
        cm = xc[:, M2_INNER + (M2_GROUPS + g) * M2_STATE:M2_INNER + (M2_GROUPS + g + 1) * M2_STATE].astype(BF16)
        cb = _dot_nt(cm, bm)
        st = st_s[g]
        y_off = jnp.dot(cm, st.astype(BF16), preferred_element_type=F32) * e_acum[:, g * 256:(g + 1) * 256]
        st_s[g] = st * jnp.exp(a_last[:, g * 256:(g + 1) * 256]) + _dot_tn(bm, xw[:, g * 256:(g + 1) * 256])
        for pp in range(2):
            p = 2 * g + pp
            a_pair = acum[:, p * 128:(p + 1) * 128]
            a_t = a_pair.T
            yd = []
            for hh in range(2):
                seg = a_pair[:, 64 * hh:64 * hh + 1] - a_t[64 * hh:64 * hh + 1, :]
                mix = jnp.where(causal, cb * jnp.exp(jnp.minimum(seg, 0.0)), 0.0).astype(BF16)
                yd.append(jnp.dot(mix, xdt_b[:, p * 128:(p + 1) * 128], preferred_element_type=F32))
            y_parts.append(jnp.where(lane < M2_HEAD_DIM, yd[0], yd[1]) + y_off[:, pp * 128:(pp + 1) * 128])
    y = jnp.concatenate(y_parts, axis=1) + d_ref[...] * xs
    y_ref[0] = _rms(y * _silu(zg_ref[0]), nrm_ref[...]).astype(y_ref.dtype)

    @pl.when(c == pl.num_programs(1) - 1)
    def _():
        for g in range(M2_GROUPS):
            st_ref[0, g * 256:(g + 1) * 256, :] = st_s[g].T
        cv_ref[0] = xbc_ref[0, q - (M2_CONV - 1):q, :]


def _ssd_prompt(xbc, dt_raw, zg, cw, cb, dtb, a, d, nrm, l):
    b, sl, _ = xbc.shape
    q = _tile(sl, M2_Q)
    tok = lambda n: pl.BlockSpec((1, q, n), lambda i, j: (i, j, 0))
    return pl.pallas_call(
        _ssd_prompt_kernel,
        grid=(b, sl // q),
        in_specs=[tok(M2_CONV_DIM), tok(M2_INNER), tok(M2_INNER),
                  _layer((M2_CONV, M2_CONV_DIM), l), _layer((1, M2_CONV_DIM), l),
                  _layer((1, M2_INNER), l), _layer((1, M2_INNER), l), _layer((1, M2_INNER), l),
                  _layer((1, M2_INNER), l)],
        out_specs=[tok(M2_INNER),
                   pl.BlockSpec((1, M2_INNER, M2_STATE), lambda i, j: (i, 0, 0)),
                   pl.BlockSpec((1, M2_CONV - 1, M2_CONV_DIM), lambda i, j: (i, 0, 0))],
        out_shape=[jax.ShapeDtypeStruct((b, sl, M2_INNER), BF16),
                   jax.ShapeDtypeStruct((b, M2_INNER, M2_STATE), F32),
                   jax.ShapeDtypeStruct((b, M2_CONV - 1, M2_CONV_DIM), F32)],
        scratch_shapes=[pltpu.VMEM((8, M2_CONV_DIM), F32), pltpu.VMEM((M2_GROUPS, M2_STATE, 256), F32)],
        compiler_params=_cp("parallel", "arbitrary"),
        name="ssd_prompt",
    )(xbc, dt_raw, zg, cw, cb, dtb, a, d, nrm)


def _ssd_step_kernel(xbc_ref, cs_ref, dt_ref, zg_ref, st_ref, cw_ref, cb_ref, dtb_ref, a_ref, d_ref, nrm_ref,
                     y_ref, sto_ref, cvo_ref):
    nb = xbc_ref.shape[0]
    x = xbc_ref[...]
    b0, b1, b2 = cs_ref[:, 0, :], cs_ref[:, 1, :], cs_ref[:, 2, :]
    conv = (cb_ref[...] + cw_ref[0:1, :] * b0 + cw_ref[1:2, :] * b1 + cw_ref[2:3, :] * b2 + cw_ref[3:4, :] * x)
    cvo_ref[:, 0, :] = b1
    cvo_ref[:, 1, :] = b2
    cvo_ref[:, 2, :] = x
    xc = _silu(conv)
    xs = xc[:, 0:M2_INNER]
    dt = _softplus(dt_ref[...] + dtb_ref[...])
    dec = jnp.exp(dt * a_ref[...])
    pad = jnp.zeros((128 - nb, M2_INNER), F32)
    xdt_t = jnp.concatenate([xs * dt, pad], axis=0).T
    dec_t = jnp.concatenate([dec, pad], axis=0).T
    hrow = lax.broadcasted_iota(jnp.int32, (M2_INNER, 1), 0)
    lane = lax.broadcasted_iota(jnp.int32, (1, 128), 1)
    ycols = jnp.zeros((M2_INNER, 128), F32)
    for b in range(nb):
        brow = jnp.where(hrow < 256, xc[b:b + 1, 512:640], xc[b:b + 1, 640:768])
        crow = jnp.where(hrow < 256, xc[b:b + 1, 768:896], xc[b:b + 1, 896:1024])
        h_new = dec_t[:, b:b + 1] * st_ref[b] + xdt_t[:, b:b + 1] * brow
        sto_ref[b] = h_new
        ycols = jnp.where(lane == b, jnp.sum(h_new * crow, axis=1, keepdims=True), ycols)
    y = ycols.T[0:nb] + d_ref[...] * xs
    y_ref[...] = _rms(y * _silu(zg_ref[...]), nrm_ref[...])


def _ssd_step(xbc, conv_state, dt_raw, zg, ssm_state, cw, cb, dtb, a, d, nrm, l):
    m = xbc.shape[0]
    nb = 8
    tok = lambda n: pl.BlockSpec((nb, n), lambda i: (i, 0))
    depth = ssm_state.shape[0]
    st = ssm_state.reshape(depth, m, M2_INNER, M2_STATE)
    return pl.pallas_call(
        _ssd_step_kernel,
        grid=(m // nb,),
        in_specs=[tok(M2_CONV_DIM),
                  pl.BlockSpec((None, nb, M2_CONV - 1, M2_CONV_DIM), lambda i: (l, i, 0, 0)),
                  tok(M2_INNER), tok(M2_INNER),
                  pl.BlockSpec((None, nb, M2_INNER, M2_STATE), lambda i: (l, i, 0, 0)),
                  _layer((M2_CONV, M2_CONV_DIM), l), _layer((1, M2_CONV_DIM), l),
                  _layer((1, M2_INNER), l), _layer((1, M2_INNER), l), _layer((1, M2_INNER), l),
                  _layer((1, M2_INNER), l)],
        out_specs=[tok(M2_INNER),
                   pl.BlockSpec((nb, M2_INNER, M2_STATE), lambda i: (i, 0, 0)),
                   pl.BlockSpec((nb, M2_CONV - 1, M2_CONV_DIM), lambda i: (i, 0, 0))],
        out_shape=[jax.ShapeDtypeStruct((m, M2_INNER), F32),
                   jax.ShapeDtypeStruct((m, M2_INNER, M2_STATE), F32),
                   jax.ShapeDtypeStruct((m, M2_CONV - 1, M2_CONV_DIM), F32)],
        compiler_params=_cp("parallel"),
        name="ssd_step",
    )(xbc, conv_state, dt_raw, zg, st, cw, cb, dtb, a, d, nrm)


def _merge_kernel(split5, h_ref, y5_ref, ysb_ref, ym_ref, gate_ref, w5_ref, wsb_ref, wm_ref, wo_ref, o_ref):
    tm = h_ref.shape[0]

    def branch(y, w_ref, lo):
        return _sigmoid(gate_ref[:, lo:lo + D_MODEL].astype(F32)) * jnp.dot(
            y.astype(BF16), w_ref[...], preferred_element_type=F32)

    if split5:
        y5 = jnp.concatenate([y5_ref[pl.ds(j, tm, stride=4), :] for j in range(4)], axis=1)
    else:
        y5 = y5_ref[...]
    merged = (branch(y5, w5_ref, 0) + branch(ysb_ref[...], wsb_ref, D_MODEL)
              + branch(ym_ref[...], wm_ref, 2 * D_MODEL))
    o_ref[...] = h_ref[...] + jnp.dot(merged.astype(BF16), wo_ref[...], preferred_element_type=F32)


def _merge(h, y5, ysb, ym, gates, w5, wsb, wm, wo, l):
    m = h.shape[0]
    tm = _tile(m, 512)
    row = lambda n: pl.BlockSpec((tm, n), lambda i: (i, 0))
    split5 = y5.shape[-1] == 128
    y5_spec = pl.BlockSpec((tm * 4, 128), lambda i: (i, 0)) if split5 else row(512)
    return pl.pallas_call(
        functools.partial(_merge_kernel, split5),
        grid=(m // tm,),
        in_specs=[row(D_MODEL), y5_spec, row(512), row(512), row(3 * D_MODEL),
                  _layer((512, D_MODEL), l), _layer((512, D_MODEL), l), _layer((512, D_MODEL), l),
                  _layer((D_MODEL, D_MODEL), l)],
        out_specs=row(D_MODEL),
        out_shape=jax.ShapeDtypeStruct((m, D_MODEL), F32),
        compiler_params=_cp("parallel"),
        name="merge",
    )(h, y5, ysb, ym, gates, w5, wsb, wm, wo)


def _mem_kv_kernel(x_ref, wk_ref, wv_ref, k_ref, v_ref, kb_ref, vb_ref):
    xb = x_ref[...].astype(BF16)
    k = jnp.dot(xb, wk_ref[...], preferred_element_type=F32)
    v = jnp.dot(xb, wv_ref[...], preferred_element_type=F32)
    k_ref[...] = k
    v_ref[...] = v
    kb_ref[...] = k.astype(BF16)
    vb_ref[...] = v.astype(BF16)


def _mem_kv(mem, wk, wv, l):
    m = mem.shape[0]
    tm = _tile(m, 512)
    row = pl.BlockSpec((tm, D_MODEL), lambda i: (i, 0))
    return pl.pallas_call(
        _mem_kv_kernel,
        grid=(m // tm,),
        in_specs=[row, _layer((D_MODEL, D_MODEL), l), _layer((D_MODEL, D_MODEL), l)],
        out_specs=[row] * 4,
        out_shape=[jax.ShapeDtypeStruct((m, D_MODEL), F32)] * 2 + [jax.ShapeDtypeStruct((m, D_MODEL), BF16)] * 2,
        compiler_params=_cp("parallel"),
        name="mem_kv",
    )(mem, wk, wv)


def _xattn_prompt_kernel(h_ref, g_ref, wq_ref, mk_ref, mv_ref, wo_ref, o_ref):
    h = h_ref[0]
    xn = _rms(h, g_ref[...]).astype(BF16)
    q = (jnp.dot(xn, wq_ref[...], preferred_element_type=F32) * (MEM_HEAD_DIM ** -0.5)).astype(BF16)
    heads = []
    for hd in range(MEM_HEADS):
        sl = slice(hd * MEM_HEAD_DIM, (hd + 1) * MEM_HEAD_DIM)
        s = _dot_nt(q[:, sl], mk_ref[0, :, sl])
        e = jnp.exp(s - jnp.max(s, axis=-1, keepdims=True))
        p = e / jnp.sum(e, axis=-1, keepdims=True)
        heads.append(jnp.dot(p.astype(BF16), mv_ref[0, :, sl], preferred_element_type=F32))
    o = jnp.concatenate(heads, axis=1).astype(BF16)
    o_ref[0] = h + jnp.dot(o, wo_ref[...], preferred_element_type=F32)


def _xattn_prompt(h, g, wq, mkb, mvb, wo, l):
    b, sl, _ = h.shape
    tq = _tile(sl, 512)
    mt = mkb.shape[1]
    return pl.pallas_call(
        _xattn_prompt_kernel,
        grid=(b, sl // tq),
        in_specs=[pl.BlockSpec((1, tq, D_MODEL), lambda i, j: (i, j, 0)),
                  _layer((1, D_MODEL), l), _layer((D_MODEL, D_MODEL), l),
                  pl.BlockSpec((1, mt, D_MODEL), lambda i, j: (i, 0, 0)),
                  pl.BlockSpec((1, mt, D_MODEL), lambda i, j: (i, 0, 0)),
                  _layer((D_MODEL, D_MODEL), l)],
        out_specs=pl.BlockSpec((1, tq, D_MODEL), lambda i, j: (i, j, 0)),
        out_shape=jax.ShapeDtypeStruct((b, sl, D_MODEL), F32),
        compiler_params=_cp("parallel", "parallel"),
        name="xattn_prompt",
    )(h, g, wq, mkb, mvb, wo)


def _xq_sample_kernel(h_ref, g_ref, wq_ref, q_ref):
    xn = _rms(h_ref[...], g_ref[...]).astype(BF16)
    q_ref[...] = jnp.dot(xn, wq_ref[...], preferred_element_type=F32) * (MEM_HEAD_DIM ** -0.5)


def _xattn_core_kernel(q_ref, mk_ref, mv_ref, o_ref):
    s = jnp.sum(mk_ref[...] * q_ref[...], axis=2, keepdims=True)
    e = jnp.exp(s - jnp.max(s, axis=0, keepdims=True))
    p = e / jnp.sum(e, axis=0, keepdims=True)
    o_ref[...] = jnp.sum(p * mv_ref[...], axis=0, keepdims=True)


def _xo_sample_kernel(h_ref, a_ref, wo_ref, o_ref):
    o_ref[...] = h_ref[...] + jnp.dot(a_ref[...].astype(BF16), wo_ref[...], preferred_element_type=F32)


def _xattn_sample(h, g, wq, cache_k, cache_v, wo, l):
    m = h.shape[0]
    mt = cache_k.shape[2]
    mat = _full((m, D_MODEL))
    q = pl.pallas_call(
        _xq_sample_kernel, grid=(1,),
        in_specs=[mat, _layer((1, D_MODEL), l), _layer((D_MODEL, D_MODEL), l)],
        out_specs=mat, out_shape=jax.ShapeDtypeStruct((m, D_MODEL), F32),
        compiler_params=_cp("arbitrary"), name="xattn_q",
    )(h, g, wq)
    per_seq = pl.BlockSpec((1, MEM_HEADS, MEM_HEAD_DIM), lambda i: (i, 0, 0))
    kv = pl.BlockSpec((None, None, mt, MEM_HEADS, MEM_HEAD_DIM), lambda i: (l, i, 0, 0, 0))
    att = pl.pallas_call(
        _xattn_core_kernel, grid=(m,),
        in_specs=[per_seq, kv, kv],
        out_specs=per_seq, out_shape=jax.ShapeDtypeStruct((m, MEM_HEADS, MEM_HEAD_DIM), F32),
        compiler_params=_cp("parallel"), name="xattn_sample",
    )(q.reshape(m, MEM_HEADS, MEM_HEAD_DIM), cache_k, cache_v)
    return pl.pallas_call(
        _xo_sample_kernel, grid=(1,),
        in_specs=[mat, mat, _layer((D_MODEL, D_MODEL), l)],
        out_specs=mat, out_shape=jax.ShapeDtypeStruct((m, D_MODEL), F32),
        compiler_params=_cp("arbitrary"), name="xattn_o",
    )(h, att.reshape(m, D_MODEL), wo)


def _mlp_kernel(h_ref, g_ref, w1_ref, w2_ref, o_ref):
    h = h_ref[...]
    xn = _rms(h, g_ref[...]).astype(BF16)
    acc = h
    for j in range(D_FF // 1024):
        a = jnp.maximum(jnp.dot(xn, w1_ref[:, j * 1024:(j + 1) * 1024], preferred_element_type=F32), 0.0)
        acc = acc + jnp.dot((a * a).astype(BF16), w2_ref[j * 1024:(j + 1) * 1024, :], preferred_element_type=F32)
    o_ref[...] = acc


def _mlp(h, g, w1, w2, l):
    m = h.shape[0]
    tm = _tile(m, 512)
    row = pl.BlockSpec((tm, D_MODEL), lambda i: (i, 0))
    return pl.pallas_call(
        _mlp_kernel,
        grid=(m // tm,),
        in_specs=[row, _layer((1, D_MODEL), l), _layer((D_MODEL, D_FF), l), _layer((D_FF, D_MODEL), l)],
        out_specs=row,
        out_shape=jax.ShapeDtypeStruct((m, D_MODEL), F32),
        compiler_params=_cp("parallel"),
        name="mlp",
    )(h, g, w1, w2)


def _norm_kernel(h_ref, g_ref, o_ref):
    o_ref[...] = _rms(h_ref[...], g_ref[...])


def _final_norm(h, g):
    m = h.shape[0]
    tm = _tile(m, 1024)
    row = pl.BlockSpec((tm, D_MODEL), lambda i: (i, 0))
    return pl.pallas_call(
        _norm_kernel,
        grid=(m // tm,),
        in_specs=[row, _full((1, D_MODEL))],
        out_specs=row,
        out_shape=jax.ShapeDtypeStruct((m, D_MODEL), F32),
        compiler_params=_cp("parallel"),
        name="final_norm",
    )(h, g)


def _block_diag(blocks):
    d, g, r, c = blocks.shape
    rep = (jnp.arange(c)[:, None] == jnp.arange(g * c)[None, :] % c).astype(blocks.dtype)
    wide = jnp.einsum("dik,kj->dij", blocks.reshape(d, g * r, c), rep, precision=lax.Precision.HIGHEST)
    on_diag = (jnp.arange(g * r)[:, None] // r) == (jnp.arange(g * c)[None, :] // c)
    return jnp.where(on_diag[None], wide, 0.0)


def kernel(x_prompt, x_sample, mem_prompt, cache_sb_k, cache_sb_v, state_s5_re, state_s5_im, state_ssm, state_conv, cache_mem_k, cache_mem_v, page_table, norm_mix, w_in, s5_lambda_re, s5_lambda_im, s5_log_dt, s5_b_re, s5_b_im, s5_c_re, s5_c_im, s5_d, s5_w_glu, s5_b_glu, sb_beta_bias, m2_conv_w, m2_conv_b, m2_dt_bias, m2_a_log, m2_d, m2_norm, w_br_s5, w_br_sb, w_br_m2, w_out, norm_mem, mem_wq, mem_wk, mem_wv, mem_wo, norm_mlp, mlp_w1, mlp_w2, norm_final):
    depth = w_in.shape[0]
    bsz, sl, _ = x_prompt.shape
    dec_b = x_sample.shape[0]
    mt = mem_prompt.shape[1]
    rep = lambda p: jnp.repeat(p, M2_HEAD_DIM, axis=-1)[:, None, :]
    row = lambda p: p[:, None, :]

    w_all = jnp.concatenate([w_in[:, :, :IN_MAIN], w_in[:, :, IN_GATE:],
                             jnp.repeat(w_in[:, :, IN_DT:IN_GATE], M2_HEAD_DIM, axis=-1)], axis=-1).astype(BF16)
    wglu, w5, wsb, wm, wo = (w.astype(BF16) for w in (s5_w_glu, w_br_s5, w_br_sb, w_br_m2, w_out))
    wq, wk, wv, wmo, w1, w2 = (w.astype(BF16) for w in (mem_wq, mem_wk, mem_wv, mem_wo, mlp_w1, mlp_w2))
    g_mix, g_mem, g_mlp = row(norm_mix), row(norm_mem), row(norm_mlp)
    s5d, bglu, cb = row(s5_d), row(s5_b_glu), row(m2_conv_b)
    dtb, a_neg, m2d, m2n = rep(m2_dt_bias), rep(-jnp.exp(m2_a_log.astype(F32))), rep(m2_d), row(m2_norm)
    tri_t = (jnp.arange(SB_TILE)[:, None] <= jnp.arange(SB_TILE)[None, :]).astype(BF16)
    bias_col = jnp.broadcast_to(sb_beta_bias[:, :, None], (depth, SB_HEADS, 128))
    h0r = state_s5_re.reshape(depth, dec_b, 2048)
    h0i = state_s5_im.reshape(depth, dec_b, 2048)
    mem2 = mem_prompt.reshape(bsz * mt, D_MODEL)

    hp = x_prompt.reshape(bsz * sl, D_MODEL)
    hs = x_sample.reshape(dec_b, D_MODEL)
    outs = [[] for _ in range(14)]
    flat = lambda p: p.reshape((depth * S5_GROUPS,) + p.shape[2:])
    wcat, poutr, pouti, pw, bbr, bbi = _s5_prep(flat(s5_lambda_re), flat(s5_lambda_im), flat(s5_log_dt),
                                                 flat(s5_b_re), flat(s5_b_im), flat(s5_c_re), flat(s5_c_im))
    power = lambda part, k: pw[:, part, k].reshape(depth, 1, 2048)
    a16r, a16i, a1r, a1i = power(0, S5_T), power(1, S5_T), power(0, 1), power(1, 1)
    per_layer = lambda p: p.reshape((depth, S5_GROUPS) + p.shape[1:])
    bbr_d = _block_diag(jnp.swapaxes(per_layer(bbr[:, :, :S5_CH]), 2, 3)).astype(BF16)
    bbi_d = _block_diag(jnp.swapaxes(per_layer(bbi[:, :, :S5_CH]), 2, 3)).astype(BF16)
    ccr_d = _block_diag(jnp.swapaxes(s5_c_re, 2, 3)).astype(BF16)
    cci_d = _block_diag(jnp.swapaxes(s5_c_im, 2, 3)).astype(BF16)

    for l in range(depth):

        u4, kt, vt, vtb, q2, k2, zg, xbc, gates, dtr = _in_proj_prompt(hp, g_mix, w_all, sb_beta_bias, l, bsz)
        r3 = lambda t: t.reshape(bsz, sl, t.shape[-1])
        y5, s5r, s5i = _s5_prompt(u4.reshape(bsz, sl * 4, 128), wcat, poutr, pouti, a16r, a16i,
                                  s5d, wglu, bglu, l)
        ysb = _sb_prompt(r3(q2), r3(k2), vtb, tri_t)
        ym, ssm, cvs = _ssd_prompt(r3(xbc), r3(dtr), r3(zg), m2_conv_w, cb, dtb, a_neg, m2d, m2n, l)
        hp = _merge(hp, y5.reshape(bsz * sl * 4, 128), ysb.reshape(bsz * sl, SB_WIDTH),
                    ym.reshape(bsz * sl, M2_INNER), gates, w5, wsb, wm, wo, l)
        mk, mv, mkb, mvb = _mem_kv(mem2, wk, wv, l)
        hp = _xattn_prompt(hp.reshape(bsz, sl, D_MODEL), g_mem, wq, mkb.reshape(bsz, mt, D_MODEL),
                           mvb.reshape(bsz, mt, D_MODEL), wmo, l).reshape(bsz * sl, D_MODEL)
        hp = _mlp(hp, g_mlp, w1, w2, l)
        for lst, val in zip(outs[:8], (
                kt, vt, s5r.reshape(bsz, S5_GROUPS, S5_STATE), s5i.reshape(bsz, S5_GROUPS, S5_STATE),
                ssm.reshape(bsz, M2_HEADS, M2_HEAD_DIM, M2_STATE), cvs,
                mk.reshape(bsz, mt, MEM_HEADS, MEM_HEAD_DIM), mv.reshape(bsz, mt, MEM_HEADS, MEM_HEAD_DIM))):
            lst.append(val)

        u, k, v, qs, zg, xbc, gates, dtr = _in_proj_sample(hs, g_mix, w_all, l)
        y5, s5r, s5i = _s5_step(u, h0r, h0i, a1r, a1i, bbr_d, bbi_d, ccr_d, cci_d, s5d, wglu, bglu, l)
        ysb = _sb_decode(qs, bias_col, cache_sb_k, cache_sb_v, page_table, l)
        ym, ssm, cvs = _ssd_step(xbc, state_conv, dtr, zg, state_ssm, m2_conv_w, cb, dtb, a_neg, m2d, m2n, l)
        hs = _merge(hs, y5, ysb, ym, gates, w5, wsb, wm, wo, l)
        hs = _xattn_sample(hs, g_mem, wq, cache_mem_k, cache_mem_v, wmo, l)
        hs = _mlp(hs, g_mlp, w1, w2, l)
        for lst, val in zip(outs[8:], (
                k.reshape(dec_b, 1, SB_HEADS, SB_HEAD_DIM), v.reshape(dec_b, 1, SB_HEADS, SB_HEAD_DIM),
                s5r.reshape(dec_b, S5_GROUPS, S5_STATE), s5i.reshape(dec_b, S5_GROUPS, S5_STATE),
                ssm.reshape(dec_b, M2_HEADS, M2_HEAD_DIM, M2_STATE), cvs)):
            lst.append(val)

    g_fin = norm_final[None, :]
    y_prompt = _final_norm(hp, g_fin).reshape(bsz, sl, D_MODEL)
    y_sample = _final_norm(hs, g_fin).reshape(dec_b, 1, D_MODEL)
    stacked = [jnp.stack(o) for o in outs]
    for i in (0, 1):
        stacked[i] = jnp.transpose(stacked[i].reshape(depth, bsz, SB_HEADS, SB_HEAD_DIM, sl), (0, 1, 4, 2, 3))
    return (y_prompt, y_sample) + tuple(stacked)
```

```python
import functools
import math

import jax
import jax.numpy as jnp
from jax import lax
from jax.experimental import pallas as pl
from jax.experimental.pallas import tpu as pltpu

F32 = jnp.float32
BF16 = jnp.bfloat16

D_MODEL = 1024
PAGE_SIZE = 128
S5_CH = 16
S5_WIDTH = 512
S5_GROUPS = 32
S5_STATE = 64
S5_LBLK = 2048
S5_T = 16
SB_WIDTH = 512
SB_HEADS = 8
SB_HEAD_DIM = 64
SB_TILE = 256
SB_QTILE = 1024
SB_UNROLL = 4
M2_INNER = 512
M2_HEADS = 8
M2_HEAD_DIM = 64
M2_GROUPS = 2
M2_STATE = 128
M2_CONV = 4
M2_CONV_DIM = 1024
M2_Q = 256
MEM_HEADS = 4
MEM_HEAD_DIM = 256
D_FF = 4096
EPS = 1e-6
LOG2E = 1.4426950408889634
SOFTPLUS_LINEAR_ABOVE = 30.0
IN_MAIN = 3584
IN_DT = 3584
IN_GATE = 3592
VMEM_LIMIT_V7X = 56 * 1024 * 1024


def _cp(*sem):
    return pltpu.CompilerParams(dimension_semantics=sem, vmem_limit_bytes=VMEM_LIMIT_V7X)


def _tile(m, pref):
    t = min(m, pref)
    while m % t:
        t -= 8
    return t


def _full(shape):
    nd = len(shape)
    return pl.BlockSpec(shape, lambda *_: (0,) * nd)


def _layer(shape, l):
    nd = len(shape)
    return pl.BlockSpec((None,) + tuple(shape), lambda *_: (l,) + (0,) * nd)


def _rms(x, g):
    return x * lax.rsqrt(jnp.mean(x * x, axis=-1, keepdims=True) + EPS) * g


def _bdot(a, b):
    return jnp.dot(a.astype(BF16), b.astype(BF16), preferred_element_type=F32)


def _dot_nt(a, b):
    return lax.dot_general(a, b, (((1,), (1,)), ((), ())), preferred_element_type=F32)


def _dot_tn(a, b):
    return lax.dot_general(a, b, (((0,), (0,)), ((), ())), preferred_element_type=F32)


def _softplus(z):
    return jnp.maximum(z, 0.0) + jnp.log(1.0 + jnp.exp(-jnp.abs(z)))


def _sigmoid(z):
    return 1.0 / (1.0 + jnp.exp(-z))


def _silu(z):
    return z * _sigmoid(z)


def _gelu_tanh(x):
    return 0.5 * x * (1.0 + jnp.tanh(math.sqrt(2.0 / math.pi) * (x + 0.044715 * (x * x * x))))


def _in_proj_common(x_ref, g_ref, w_ref, zg_ref, xbc_ref, gate_ref, dt_ref):
    xn = _rms(x_ref[...], g_ref[...]).astype(BF16)

    def mm(lo, hi):
        return jnp.dot(xn, w_ref[:, lo:hi], preferred_element_type=F32)

    zg_ref[...] = mm(2048, 2560)
    xbc_ref[...] = mm(2560, 3584)
    gate_ref[...] = mm(3584, 6656).astype(gate_ref.dtype)
    dt_ref[...] = mm(6656, 7168)
    return mm


def _in_proj_prompt_kernel(l, n_alias, bias_ref, x_ref, g_ref, w_ref, *rest):
    u4_ref, kt_ref, vt_ref, vtb_ref, q2_ref, k2_ref, zg_ref, xbc_ref, gate_ref, dt_ref = rest[n_alias:]
    mm = _in_proj_common(x_ref, g_ref, w_ref, zg_ref, xbc_ref, gate_ref, dt_ref)
    tm = x_ref.shape[0]
    u = mm(0, 512)
    for j in range(4):
        u4_ref[pl.ds(j, tm, stride=4), :] = u[:, j * 128:(j + 1) * 128]
    q = mm(512, 1024) * (SB_HEAD_DIM ** -0.5)
    k = mm(1024, 1536)
    v = mm(1536, 2048)
    kt_ref[...] = k.T
    vt_ref[...] = v.T
    vtb_ref[...] = v.astype(BF16).T
    lane = lax.broadcasted_iota(jnp.int32, (tm, 128), 1)
    for h in range(SB_HEADS):
        sl = slice((h // 2) * 128, (h // 2 + 1) * 128)
        qh, kh = q[:, sl], k[:, sl]
        if h % 2:
            qh, kh = pltpu.roll(qh, 64, 1), pltpu.roll(kh, 64, 1)
        extra = lane == SB_HEAD_DIM
        q2_ref[:, h * 128:(h + 1) * 128] = jnp.where(
            lane < SB_HEAD_DIM, qh, jnp.where(extra, bias_ref[l, h], 0.0)).astype(BF16)
        k2_ref[:, h * 128:(h + 1) * 128] = jnp.where(
            lane < SB_HEAD_DIM, kh, jnp.where(extra, 1.0, 0.0)).astype(BF16)


def _in_proj_prompt(x, g, w_all, bias, l, bsz, kv_all):
    m = x.shape[0]
    depth = w_all.shape[0]
    sl = m // bsz
    tm = _tile(sl, 256)
    per_b = sl // tm
    row = lambda n: pl.BlockSpec((tm, n), lambda i: (i, 0))
    tr = pl.BlockSpec((None, 512, tm), lambda i: (i // per_b, 0, i % per_b))
    tr_all = pl.BlockSpec((None, None, 512, tm), lambda i: (l, i // per_b, 0, i % per_b))
    outs = [(1024, BF16), (1024, BF16), (512, F32), (1024, F32), (3072, BF16), (512, F32)]
    kv_all = () if kv_all is None else tuple(kv_all)
    return pl.pallas_call(
        functools.partial(_in_proj_prompt_kernel, l, len(kv_all)),
        grid=(m // tm,),
        in_specs=[pl.BlockSpec(memory_space=pltpu.SMEM),
                  row(D_MODEL), _layer((1, D_MODEL), l), _layer((D_MODEL, 7168), l)]
                 + [pl.BlockSpec(memory_space=pl.ANY)] * len(kv_all),
        out_specs=[pl.BlockSpec((tm * 4, 128), lambda i: (i, 0)), tr_all, tr_all, tr] + [row(n) for n, _ in outs],
        out_shape=[jax.ShapeDtypeStruct((m * 4, 128), F32),
                   jax.ShapeDtypeStruct((depth, bsz, 512, sl), F32),
                   jax.ShapeDtypeStruct((depth, bsz, 512, sl), F32),
                   jax.ShapeDtypeStruct((bsz, 512, sl), BF16)]
                  + [jax.ShapeDtypeStruct((m, n), dt) for n, dt in outs],
        input_output_aliases={4 + i: 1 + i for i in range(len(kv_all))},
        compiler_params=_cp("parallel"),
        name="in_proj",
    )(bias, x, g, w_all, *kv_all)


def _in_proj_sample_kernel(x_ref, g_ref, w_ref, u_ref, k_ref, v_ref, q_ref, zg_ref, xbc_ref, gate_ref, dt_ref):
    mm = _in_proj_common(x_ref, g_ref, w_ref, zg_ref, xbc_ref, gate_ref, dt_ref)
    u_ref[...] = mm(0, 512)
    q_ref[...] = mm(512, 1024) * (SB_HEAD_DIM ** -0.5)
    k_ref[...] = mm(1024, 1536)
    v_ref[...] = mm(1536, 2048)


def _in_proj_sample(x, g, w_all, l):
    m = x.shape[0]
    outs = [512, 512, 512, 512, 512, 1024, 3072, 512]
    return pl.pallas_call(
        _in_proj_sample_kernel,
        grid=(1,),
        in_specs=[_full((m, D_MODEL)), _layer((1, D_MODEL), l), _layer((D_MODEL, 7168), l)],
        out_specs=[_full((m, n)) for n in outs],
        out_shape=[jax.ShapeDtypeStruct((m, n), F32) for n in outs],
        compiler_params=_cp("arbitrary"),
        name="in_proj_step",
    )(x, g, w_all)


def _s5_prep_kernel(lr_row, li_row, lr_col, li_col, ldt_ref, btr_ref, bti_ref, ctr_ref, cti_ref,
                    wcat_ref, poutr_ref, pouti_ref, pw_ref, bbr_ref, bbi_ref):
    dt = jnp.exp(ldt_ref[0])
    lr, li = lr_row[0], li_row[0]
    kk = lax.broadcasted_iota(jnp.int32, (24, 1), 0).astype(F32)
    mag = jnp.exp(kk * (lr * dt))
    ang = kk * (li * dt)
    p_re, p_im = mag * jnp.cos(ang), mag * jnp.sin(ang)
    pw_ref[0, 0] = p_re
    pw_ref[0, 1] = p_im

    def rep(p, lo):
        return jnp.broadcast_to(p[lo:lo + S5_T][:, None, :], (S5_T, S5_CH, S5_STATE)).reshape(256, S5_STATE)

    c_re, c_im = ctr_ref[0], cti_ref[0]
    pr0, pi0 = rep(p_re, 0), rep(p_im, 0)
    l_re = c_re * pr0 - c_im * pi0
    l_im = c_re * pi0 + c_im * pr0
    pr1, pi1 = rep(p_re, 1), rep(p_im, 1)
    poutr_ref[0] = (c_re * pr1 - c_im * pi1).astype(BF16)
    pouti_ref[0] = (-(c_re * pi1 + c_im * pr1)).astype(BF16)

    lrc, lic = lr_col[0], li_col[0]
    m1 = jnp.exp(lrc * dt)
    a_re, a_im = m1 * jnp.cos(lic * dt), m1 * jnp.sin(lic * dt)
    den = lrc * lrc + lic * lic
    f_re = ((a_re - 1.0) * lrc + a_im * lic) / den
    f_im = (a_im * lrc - (a_re - 1.0) * lic) / den
    b_re, b_im = btr_ref[0], bti_ref[0]
    bb_re = f_re * b_re - f_im * b_im
    bb_im = f_re * b_im + f_im * b_re
    bbr_ref[0] = bb_re
    bbi_ref[0] = bb_im

    hp = lax.Precision.HIGHEST
    kt = (jnp.dot(l_re, bb_re, precision=hp, preferred_element_type=F32)
          - jnp.dot(l_im, bb_im, precision=hp, preferred_element_type=F32))
    lane_s = lax.broadcasted_iota(jnp.int32, (1, 256), 1) // S5_CH
    toep = jnp.where(lane_s == 0, kt, 0.0)
    for s in range(1, S5_T):
        sh = jnp.concatenate([jnp.zeros((S5_CH * s, 256), F32), kt[:256 - S5_CH * s]], axis=0)
        toep = jnp.where(lane_s == s, sh, toep)

    ks = (S5_T - 1 - lane_s).astype(F32)
    magc = jnp.exp(ks * (lrc * dt))
    angc = ks * (lic * dt)
    q_re, q_im = magc * jnp.cos(angc), magc * jnp.sin(angc)
    wcat_ref[0, 0:256] = toep.astype(BF16)
    wcat_ref[0, 256:320] = (q_re * bb_re - q_im * bb_im).astype(BF16)
    wcat_ref[0, 320:384] = (q_re * bb_im + q_im * bb_re).astype(BF16)


def _s5_prep(lam_re, lam_im, log_dt, b_re, b_im, c_re, c_im):
    g = lam_re.shape[0]
    per = lambda *shape: pl.BlockSpec((1,) + shape, lambda i: (i,) + (0,) * len(shape))
    tile_b = lambda b: jnp.tile(b, (1, 1, S5_T))
    tile_c = lambda c: jnp.tile(c, (1, S5_T, 1))
    return pl.pallas_call(
        _s5_prep_kernel,
        grid=(g,),
        in_specs=[per(1, 64), per(1, 64), per(64, 1), per(64, 1), per(1, 1),
                  per(64, 256), per(64, 256), per(256, 64), per(256, 64)],
        out_specs=[per(384, 256), per(256, 64), per(256, 64), per(2, 24, 64), per(64, 256), per(64, 256)],
        out_shape=[jax.ShapeDtypeStruct((g, 384, 256), BF16),
                   jax.ShapeDtypeStruct((g, 256, 64), BF16),
                   jax.ShapeDtypeStruct((g, 256, 64), BF16),
                   jax.ShapeDtypeStruct((g, 2, 24, 64), F32),
                   jax.ShapeDtypeStruct((g, 64, 256), F32),
                   jax.ShapeDtypeStruct((g, 64, 256), F32)],
        compiler_params=_cp("parallel"),
        name="s5_prep",
    )(lam_re[:, None, :], lam_im[:, None, :], lam_re[:, :, None], lam_im[:, :, None],
      log_dt[:, None, None], tile_b(b_re), tile_b(b_im), tile_c(c_re), tile_c(c_im))


def _rows(u_ref, s, j, n):
    return u_ref[0, pl.ds(s * 4 + j, n, stride=4 * S5_T), :]


def _s5_prompt_kernel(u_ref, wcat_ref, poutr_ref, pouti_ref, ar_ref, ai_ref, d_ref, wglu_ref, bglu_ref,
                      y_ref, sr_ref, si_ref, v_s, yt_s, hr_s, hi_s, cr_s, ci_s):
    n = v_s.shape[2]
    lb = pl.program_id(1)

    @pl.when(lb == 0)
    def _():
        cr_s[...] = jnp.zeros_like(cr_s)
        ci_s[...] = jnp.zeros_like(ci_s)

    for s in range(S5_T):
        for j in range(4):
            v_s[s, j * 128:(j + 1) * 128, :] = _rows(u_ref, s, j, n).T.astype(BF16)

    for gp in range(S5_GROUPS // 2):
        res = []
        for g in (2 * gp, 2 * gp + 1):
            vg = v_s[:, g * S5_CH:(g + 1) * S5_CH, :].reshape(S5_T * S5_CH, n)
            r = jnp.dot(wcat_ref[g], vg, preferred_element_type=F32)
            yt_s[g] = r[0:256]
            res.append(r)
        hr_s[:, gp * 128:(gp + 1) * 128] = jnp.concatenate([res[0][256:320], res[1][256:320]], axis=0).T
        hi_s[:, gp * 128:(gp + 1) * 128] = jnp.concatenate([res[0][320:384], res[1][320:384]], axis=0).T

    a_re, a_im = ar_ref[...], ai_ref[...]

    def step(i, carry):
        h_re, h_im = carry
        s_re = hr_s[pl.ds(i, 1), :]
        s_im = hi_s[pl.ds(i, 1), :]
        hr_s[pl.ds(i, 1), :] = h_re
        hi_s[pl.ds(i, 1), :] = h_im
        return (a_re * h_re - a_im * h_im + s_re, a_re * h_im + a_im * h_re + s_im)

    h_re, h_im = lax.fori_loop(0, n, step, (cr_s[...], ci_s[...]))
    cr_s[...] = h_re
    ci_s[...] = h_im
    sr_ref[0] = h_re
    si_ref[0] = h_im

    for gp in range(S5_GROUPS // 2):
        ht_re = hr_s[:, gp * 128:(gp + 1) * 128].T.astype(BF16)
        ht_im = hi_s[:, gp * 128:(gp + 1) * 128].T.astype(BF16)
        for j, g in enumerate((2 * gp, 2 * gp + 1)):
            yt_s[g] += (jnp.dot(poutr_ref[g], ht_re[64 * j:64 * j + 64], preferred_element_type=F32)
                        + jnp.dot(pouti_ref[g], ht_im[64 * j:64 * j + 64], preferred_element_type=F32))

    for t in range(S5_T):
        yt = yt_s[:, t * S5_CH:(t + 1) * S5_CH, :].reshape(S5_WIDTH, n).T
        y = yt + d_ref[...] * jnp.concatenate([_rows(u_ref, t, j, n) for j in range(4)], axis=1)
        gl = _gelu_tanh(y)
        o = gl * _sigmoid(jnp.dot(gl.astype(BF16), wglu_ref[...], preferred_element_type=F32) + bglu_ref[...])
        for j in range(4):
            y_ref[0, pl.ds(t * 4 + j, n, stride=4 * S5_T), :] = o[:, j * 128:(j + 1) * 128]


def _s5_prompt(u4, wcat, poutr, pouti, a16r, a16i, d, wglu, bglu, l):
    b, sl = u4.shape[0], u4.shape[1] // 4
    lblk = _tile(sl, S5_LBLK)
    n = lblk // S5_T
    grp = lambda r, c: pl.BlockSpec((S5_GROUPS, r, c), lambda i, j: (l, 0, 0))
    return pl.pallas_call(
        _s5_prompt_kernel,
        grid=(b, sl // lblk),
        in_specs=[pl.BlockSpec((1, lblk * 4, 128), lambda i, j: (i, j, 0)),
                  grp(384, 256), grp(256, 64), grp(256, 64),
                  _layer((1, 2048), l), _layer((1, 2048), l),
                  _layer((1, S5_WIDTH), l), _layer((S5_WIDTH, S5_WIDTH), l), _layer((1, S5_WIDTH), l)],
        out_specs=[pl.BlockSpec((1, lblk * 4, 128), lambda i, j: (i, j, 0)),
                   pl.BlockSpec((1, 1, 2048), lambda i, j: (i, 0, 0)),
                   pl.BlockSpec((1, 1, 2048), lambda i, j: (i, 0, 0))],
        out_shape=[jax.ShapeDtypeStruct((b, sl * 4, 128), F32),
                   jax.ShapeDtypeStruct((b, 1, 2048), F32),
                   jax.ShapeDtypeStruct((b, 1, 2048), F32)],
        scratch_shapes=[pltpu.VMEM((S5_T, S5_WIDTH, n), BF16),
                        pltpu.VMEM((S5_GROUPS, 256, n), F32),
                        pltpu.VMEM((n, 2048), F32), pltpu.VMEM((n, 2048), F32),
                        pltpu.VMEM((1, 2048), F32), pltpu.VMEM((1, 2048), F32)],
        compiler_params=_cp("parallel", "arbitrary"),
        name="s5_prompt",
    )(u4, wcat, poutr, pouti, a16r, a16i, d, wglu, bglu)


def _s5_step_kernel(u_ref, h0r_ref, h0i_ref, ar_ref, ai_ref, bbr_ref, bbi_ref, ccr_ref, cci_ref,
                    d_ref, wglu_ref, bglu_ref, y_ref, hr_ref, hi_ref):
    u = u_ref[...]
    ub = u.astype(BF16)
    a_re, a_im = ar_ref[...], ai_ref[...]
    h0r, h0i = h0r_ref[...], h0i_ref[...]
    h_re = a_re * h0r - a_im * h0i + jnp.dot(ub, bbr_ref[...], preferred_element_type=F32)
    h_im = a_re * h0i + a_im * h0r + jnp.dot(ub, bbi_ref[...], preferred_element_type=F32)
    hr_ref[...] = h_re
    hi_ref[...] = h_im
    y = (jnp.dot(h_re.astype(BF16), ccr_ref[...], preferred_element_type=F32)
         - jnp.dot(h_im.astype(BF16), cci_ref[...], preferred_element_type=F32) + d_ref[...] * u)
    gl = _gelu_tanh(y)
    y_ref[...] = gl * _sigmoid(jnp.dot(gl.astype(BF16), wglu_ref[...], preferred_element_type=F32) + bglu_ref[...])


def _s5_step(u, h0r, h0i, a1r, a1i, bbr, bbi, ccr, cci, d, wglu, bglu, l):
    m = u.shape[0]
    return pl.pallas_call(
        _s5_step_kernel,
        grid=(1,),
        in_specs=[_full((m, S5_WIDTH)), _layer((m, 2048), l), _layer((m, 2048), l),
                  _layer((1, 2048), l), _layer((1, 2048), l),
                  _layer((S5_WIDTH, 2048), l), _layer((S5_WIDTH, 2048), l),
                  _layer((2048, S5_WIDTH), l), _layer((2048, S5_WIDTH), l),
                  _layer((1, S5_WIDTH), l), _layer((S5_WIDTH, S5_WIDTH), l), _layer((1, S5_WIDTH), l)],
        out_specs=[_full((m, S5_WIDTH)), _full((m, 2048)), _full((m, 2048))],
        out_shape=[jax.ShapeDtypeStruct((m, S5_WIDTH), F32),
                   jax.ShapeDtypeStruct((m, 2048), F32), jax.ShapeDtypeStruct((m, 2048), F32)],
        compiler_params=_cp("arbitrary"),
        name="s5_step",
    )(u, h0r, h0i, a1r, a1i, bbr, bbi, ccr, cci, d, wglu, bglu)


def _sb_prompt_kernel(q_ref, k_ref, vt_ref, tri_ref, o_ref, acc_s, c_s):
    tq, tk, tc = q_ref.shape[1], SB_TILE, 256
    per = tq // tk
    qi = pl.program_id(2)
    chunks = [(hh, a) for hh in range(2) for a in range(tq // tc)]
    qs = [q_ref[0, a * tc:(a + 1) * tc, hh * 128:(hh + 1) * 128] for hh, a in chunks]
    key = lax.broadcasted_iota(jnp.int32, (tk, tc), 0)
    qry = lax.broadcasted_iota(jnp.int32, (tk, tc), 1)
    acc_s[...] = jnp.zeros_like(acc_s)
    c_s[...] = jnp.zeros_like(c_s)

    def block(kb, off):
        start = pl.multiple_of(kb * tk, tk)
        kblk = k_ref[0, pl.ds(start, tk), :]
        vt = vt_ref[0, :, pl.ds(start, tk)]
        live, masks = [], {}
        for i, (hh, a) in enumerate(chunks):
            if off is not None and off >= (a + 1) * tc - 1:
                continue
            live.append(i)
            if off is not None and off + tk - 1 >= a * tc:
                masks[i] = key + (off - a * tc) < qry
        def logits(i):
            hh = chunks[i][0]
            return _dot_nt(kblk[:, hh * 128:(hh + 1) * 128], qs[i])

        zs, rs = {live[0]: logits(live[0])}, {}
        for n_done, i in enumerate(live):
            if n_done + 1 < len(live):
                zs[live[n_done + 1]] = logits(live[n_done + 1])
            z = zs[i]
            sp = jnp.where(z > SOFTPLUS_LINEAR_ABOVE, z, jnp.log(1.0 + jnp.exp2(z * LOG2E)))
            if i in masks:
                sp = jnp.where(masks[i], sp, 0.0)
            rs[i] = jnp.dot(tri_ref[...], sp.astype(BF16), preferred_element_type=F32)
        for i in live:
            w = jnp.exp(zs[i] - rs[i])
            if i in masks:
                w = jnp.where(masks[i], w, 0.0)
            c = c_s[i]
            acc_s[i] += jnp.exp(-c) * jnp.dot(vt, w.astype(BF16), preferred_element_type=F32)
            c_s[i] = c + rs[i][0:1, :]

    for d in range(per - 1, -1, -1):
        block(qi * per + d, d * tk)
    n = qi * per
    trips = n // SB_UNROLL

    @pl.loop(0, trips)
    def _(j):
        for d in range(SB_UNROLL):
            block(n - 1 - j * SB_UNROLL - d, None)

    assert per % SB_UNROLL == 0 or SB_UNROLL == 2 * per
    if per % SB_UNROLL:
        @pl.when(n - trips * SB_UNROLL > 0)
        def _():
            for d in range(per - 1, -1, -1):
                block(d, None)

    chan = lax.broadcasted_iota(jnp.int32, (128, tc), 0)
    for a in range(tq // tc):
        o_ref[0, a * tc:(a + 1) * tc, :] = jnp.where(
            chan < SB_HEAD_DIM, acc_s[a], acc_s[tq // tc + a]).T.astype(o_ref.dtype)


def _sb_prompt(q2, k2, vtb, tri_t):
    b, _, sl = vtb.shape
    tq = _tile(sl, SB_QTILE)
    return pl.pallas_call(
        _sb_prompt_kernel,
        grid=(b, SB_HEADS // 2, sl // tq),
        in_specs=[pl.BlockSpec((1, tq, 256), lambda i, p, j: (i, j, p)),
                  pl.BlockSpec((1, sl, 256), lambda i, p, j: (i, 0, p)),
                  pl.BlockSpec((1, 128, sl), lambda i, p, j: (i, p, 0)),
                  _full((SB_TILE, SB_TILE))],
        out_specs=pl.BlockSpec((1, tq, 128), lambda i, p, j: (i, j, p)),
        out_shape=jax.ShapeDtypeStruct((b, sl, SB_WIDTH), BF16),
        scratch_shapes=[pltpu.VMEM((2 * tq // 256, 128, 256), F32), pltpu.VMEM((2 * tq // 256, 1, 256), F32)],
        compiler_params=_cp("parallel", "parallel", "arbitrary"),
        name="sb_prompt",
    )(q2, k2, vtb, tri_t)


def _split_bf16(x):
    hi = x.astype(BF16)
    return hi, (x - hi.astype(F32)).astype(BF16)


def _sb_decode_kernel(npg, pt_ref, q_ref, bias_ref, tri_ref, later_ref, *refs):
    k_refs, v_refs = refs[:npg], refs[npg:2 * npg]
    o_ref, qc_s, acc_s, c_s = refs[2 * npg:]
    j = pl.program_id(1)

    @pl.when(j == 0)
    def _():
        qc_s[...] = jnp.broadcast_to(q_ref[0], (128, SB_WIDTH)).T
        acc_s[...] = jnp.zeros_like(acc_s)
        c_s[...] = jnp.zeros_like(c_s)

    qc = qc_s[...]
    bias = jnp.concatenate([bias_ref[...]] * npg, axis=0)
    z = jnp.concatenate(
        [(k_refs[i][...] * qc).reshape(SB_HEADS, SB_HEAD_DIM, PAGE_SIZE).sum(axis=1) for i in range(npg)],
        axis=0) + bias
    lk = -_softplus(z)
    hi, lo = _split_bf16(lk)
    r_loc = jnp.dot(jnp.concatenate([hi, lo], axis=1), tri_ref[...], preferred_element_type=F32)
    tot = jnp.broadcast_to(r_loc[:, 0:1], r_loc.shape)
    hi, lo = _split_bf16(tot)
    c_in = c_s[...]
    r = (r_loc + jnp.dot(later_ref[...], jnp.concatenate([hi, lo], axis=0), preferred_element_type=F32)
         + jnp.concatenate([c_in] * npg, axis=0))
    w = jnp.exp(z + r)
    c_s[...] = r[0:SB_HEADS] - r_loc[0:SB_HEADS] + tot[0:SB_HEADS]
    acc = acc_s[...]
    for i in range(npg):
        w_rows = jnp.broadcast_to(w[i * SB_HEADS:(i + 1) * SB_HEADS][:, None, :],
                                  (SB_HEADS, SB_HEAD_DIM, PAGE_SIZE)).reshape(SB_WIDTH, PAGE_SIZE)
        acc = acc + v_refs[i][...] * w_rows
    acc_s[...] = acc

    @pl.when(j == pl.num_programs(1) - 1)
    def _():
        o_ref[0] = jnp.sum(acc.T, axis=0, keepdims=True)


def _sb_decode(q, bias_col, cache_k, cache_v, page_table, l):
    m = q.shape[0]
    n_pages = page_table.shape[1]
    npg = 16
    while n_pages % npg:
        npg //= 2
    steps = n_pages // npg
    rows = npg * SB_HEADS

    def page(i):
        return pl.BlockSpec((None, None, SB_WIDTH, PAGE_SIZE),
                            lambda b, j, pt: (l, pt[b, (steps - 1 - j) * npg + i], 0, 0))

    const = lambda shape: pl.BlockSpec(shape, lambda b, j, pt: (0,) * len(shape))
    grid_spec = pltpu.PrefetchScalarGridSpec(
        num_scalar_prefetch=1,
        grid=(m, steps),
        in_specs=[pl.BlockSpec((1, 1, SB_WIDTH), lambda b, j, pt: (b, 0, 0)),
                  pl.BlockSpec((None, SB_HEADS, 128), lambda b, j, pt: (l, 0, 0)),
                  const((2 * PAGE_SIZE, PAGE_SIZE)), const((rows, 2 * rows))]
                 + [page(i) for i in range(npg)] * 2,
        out_specs=pl.BlockSpec((1, 1, SB_WIDTH), lambda b, j, pt: (b, 0, 0)),
        scratch_shapes=[pltpu.VMEM((SB_WIDTH, 128), F32), pltpu.VMEM((SB_WIDTH, PAGE_SIZE), F32),
                        pltpu.VMEM((SB_HEADS, 128), F32)],
    )
    depth, n_pool = cache_k.shape[:2]
    ck = jnp.transpose(cache_k, (0, 1, 3, 4, 2)).reshape(depth, n_pool, SB_WIDTH, PAGE_SIZE)
    cv = jnp.transpose(cache_v, (0, 1, 3, 4, 2)).reshape(depth, n_pool, SB_WIDTH, PAGE_SIZE)
    tri = (jnp.arange(2 * PAGE_SIZE)[:, None] % PAGE_SIZE >= jnp.arange(PAGE_SIZE)[None, :]).astype(BF16)
    ridx = jnp.arange(rows)
    later = ((ridx[None, :] % SB_HEADS == ridx[:, None] % SB_HEADS)
             & (ridx[None, :] // SB_HEADS > ridx[:, None] // SB_HEADS))
    later = jnp.concatenate([later, later], axis=1).astype(BF16)
    out = pl.pallas_call(
        functools.partial(_sb_decode_kernel, npg),
        grid_spec=grid_spec,
        out_shape=jax.ShapeDtypeStruct((m, 1, SB_WIDTH), F32),
        compiler_params=_cp("parallel", "arbitrary"),
        name="sb_decode",
    )(page_table, q.reshape(m, 1, SB_WIDTH), bias_col, tri, later, *([ck] * npg), *([cv] * npg))
    return out.reshape(m, SB_WIDTH)


def _cumsum_rows(x):
    n = x.shape[0]
    row = lax.broadcasted_iota(jnp.int32, (n, 1), 0)
    k = 1
    while k < n:
        x = x + jnp.where(row >= k, pltpu.roll(x, k, 0), 0.0)
        k *= 2
    return x


def _ssd_prompt_kernel(xbc_ref, dt_ref, zg_ref, cw_ref, cb_ref, dtb_ref, a_ref, d_ref, nrm_ref,
                       y_ref, st_ref, cv_ref, prev_s, st_s):
    q = xbc_ref.shape[1]
    c = pl.program_id(1)

    @pl.when(c == 0)
    def _():
        prev_s[...] = jnp.zeros_like(prev_s)
        st_s[...] = jnp.zeros_like(st_s)

    u = xbc_ref[0]
    prev = prev_s[...]
    row8 = lax.broadcasted_iota(jnp.int32, (8, 1), 0)
    conv = cb_ref[...] + cw_ref[M2_CONV - 1:M2_CONV, :] * u
    for k in range(1, M2_CONV):
        ru = pltpu.roll(u, k, 0)
        top = jnp.where(row8 < k, pltpu.roll(prev, k, 0), ru[0:8])
        conv = conv + cw_ref[M2_CONV - 1 - k:M2_CONV - k, :] * jnp.concatenate([top, ru[8:]], axis=0)
    prev_s[...] = u[q - 8:q]
    xc = _silu(conv)
    xs = xc[:, 0:M2_INNER]
    dt = _softplus(dt_ref[0] + dtb_ref[...])
    acum = _cumsum_rows(dt * a_ref[...])
    a_last = acum[q - 1:q, :]
    xdt = xs * dt
    xw = (xdt * jnp.exp(a_last - acum)).astype(BF16)
    xdt_b = xdt.astype(BF16)
    e_acum = jnp.exp(acum)
    rows = lax.broadcasted_iota(jnp.int32, (q, q), 0)
    cols = lax.broadcasted_iota(jnp.int32, (q, q), 1)
    causal = cols <= rows
    lane = lax.broadcasted_iota(jnp.int32, (q, 128), 1)
    y_parts = []
    for g in range(M2_GROUPS):
        bm = xc[:, M2_INNER + g * M2_STATE:M2_INNER + (g + 1) * M2_STATE].astype(BF16)
        cm = xc[:, M2_INNER + (M2_GROUPS + g) * M2_STATE:M2_INNER + (M2_GROUPS + g + 1) * M2_STATE].astype(BF16)
        cb = _dot_nt(cm, bm)
        st = st_s[g]
        y_off = jnp.dot(cm, st.astype(BF16), preferred_element_type=F32) * e_acum[:, g * 256:(g + 1) * 256]
        st_s[g] = st * jnp.exp(a_last[:, g * 256:(g + 1) * 256]) + _dot_tn(bm, xw[:, g * 256:(g + 1) * 256])
        for pp in range(2):
            p = 2 * g + pp
            a_pair = acum[:, p * 128:(p + 1) * 128]
            a_t = a_pair.T
            yd = []
            for hh in range(2):
                seg = a_pair[:, 64 * hh:64 * hh + 1] - a_t[64 * hh:64 * hh + 1, :]
                mix = jnp.where(causal, cb * jnp.exp(jnp.minimum(seg, 0.0)), 0.0).astype(BF16)
                yd.append(jnp.dot(mix, xdt_b[:, p * 128:(p + 1) * 128], preferred_element_type=F32))
            y_parts.append(jnp.where(lane < M2_HEAD_DIM, yd[0], yd[1]) + y_off[:, pp * 128:(pp + 1) * 128])
    y = jnp.concatenate(y_parts, axis=1) + d_ref[...] * xs
    y_ref[0] = _rms(y * _silu(zg_ref[0]), nrm_ref[...]).astype(y_ref.dtype)

    @pl.when(c == pl.num_programs(1) - 1)
    def _():
        for g in range(M2_GROUPS):
            st_ref[0, g * 256:(g + 1) * 256, :] = st_s[g].T
        cv_ref[0] = xbc_ref[0, q - (M2_CONV - 1):q, :]


def _ssd_prompt(xbc, dt_raw, zg, cw, cb, dtb, a, d, nrm, l):
    b, sl, _ = xbc.shape
    q = _tile(sl, M2_Q)
    tok = lambda n: pl.BlockSpec((1, q, n), lambda i, j: (i, j, 0))
    return pl.pallas_call(
        _ssd_prompt_kernel,
        grid=(b, sl // q),
        in_specs=[tok(M2_CONV_DIM), tok(M2_INNER), tok(M2_INNER),
                  _layer((M2_CONV, M2_CONV_DIM), l), _layer((1, M2_CONV_DIM), l),
                  _layer((1, M2_INNER), l), _layer((1, M2_INNER), l), _layer((1, M2_INNER), l),
                  _layer((1, M2_INNER), l)],
        out_specs=[tok(M2_INNER),
                   pl.BlockSpec((1, M2_INNER, M2_STATE), lambda i, j: (i, 0, 0)),
                   pl.BlockSpec((1, M2_CONV - 1, M2_CONV_DIM), lambda i, j: (i, 0, 0))],
        out_shape=[jax.ShapeDtypeStruct((b, sl, M2_INNER), BF16),
                   jax.ShapeDtypeStruct((b, M2_INNER, M2_STATE), F32),
                   jax.ShapeDtypeStruct((b, M2_CONV - 1, M2_CONV_DIM), F32)],
        scratch_shapes=[pltpu.VMEM((8, M2_CONV_DIM), F32), pltpu.VMEM((M2_GROUPS, M2_STATE, 256), F32)],
        compiler_params=_cp("parallel", "arbitrary"),
        name="ssd_prompt",
    )(xbc, dt_raw, zg, cw, cb, dtb, a, d, nrm)


def _ssd_step_kernel(xbc_ref, cs_ref, dt_ref, zg_ref, st_ref, cw_ref, cb_ref, dtb_ref, a_ref, d_ref, nrm_ref,
                     y_ref, sto_ref, cvo_ref):
    nb = xbc_ref.shape[0]
    x = xbc_ref[...]
    b0, b1, b2 = cs_ref[:, 0, :], cs_ref[:, 1, :], cs_ref[:, 2, :]
    conv = (cb_ref[...] + cw_ref[0:1, :] * b0 + cw_ref[1:2, :] * b1 + cw_ref[2:3, :] * b2 + cw_ref[3:4, :] * x)
    cvo_ref[:, 0, :] = b1
    cvo_ref[:, 1, :] = b2
    cvo_ref[:, 2, :] = x
    xc = _silu(conv)
    xs = xc[:, 0:M2_INNER]
    dt = _softplus(dt_ref[...] + dtb_ref[...])
    dec = jnp.exp(dt * a_ref[...])
    pad = jnp.zeros((128 - nb, M2_INNER), F32)
    xdt_t = jnp.concatenate([xs * dt, pad], axis=0).T
    dec_t = jnp.concatenate([dec, pad], axis=0).T
    hrow = lax.broadcasted_iota(jnp.int32, (M2_INNER, 1), 0)
    lane = lax.broadcasted_iota(jnp.int32, (1, 128), 1)
    ycols = jnp.zeros((M2_INNER, 128), F32)
    for b in range(nb):
        brow = jnp.where(hrow < 256, xc[b:b + 1, 512:640], xc[b:b + 1, 640:768])
        crow = jnp.where(hrow < 256, xc[b:b + 1, 768:896], xc[b:b + 1, 896:1024])
        h_new = dec_t[:, b:b + 1] * st_ref[b] + xdt_t[:, b:b + 1] * brow
        sto_ref[b] = h_new
        ycols = jnp.where(lane == b, jnp.sum(h_new * crow, axis=1, keepdims=True), ycols)
    y = ycols.T[0:nb] + d_ref[...] * xs
    y_ref[...] = _rms(y * _silu(zg_ref[...]), nrm_ref[...])


def _ssd_step(xbc, conv_state, dt_raw, zg, ssm_state, cw, cb, dtb, a, d, nrm, l):
    m = xbc.shape[0]
    nb = 8
    tok = lambda n: pl.BlockSpec((nb, n), lambda i: (i, 0))
    depth = ssm_state.shape[0]
    st = ssm_state.reshape(depth, m, M2_INNER, M2_STATE)
    return pl.pallas_call(
        _ssd_step_kernel,
        grid=(m // nb,),
        in_specs=[tok(M2_CONV_DIM),
                  pl.BlockSpec((None, nb, M2_CONV - 1, M2_CONV_DIM), lambda i: (l, i, 0, 0)),
                  tok(M2_INNER), tok(M2_INNER),
                  pl.BlockSpec((None, nb, M2_INNER, M2_STATE), lambda i: (l, i, 0, 0)),
                  _layer((M2_CONV, M2_CONV_DIM), l), _layer((1, M2_CONV_DIM), l),
                  _layer((1, M2_INNER), l), _layer((1, M2_INNER), l), _layer((1, M2_INNER), l),
                  _layer((1, M2_INNER), l)],
        out_specs=[tok(M2_INNER),
                   pl.BlockSpec((nb, M2_INNER, M2_STATE), lambda i: (i, 0, 0)),
                   pl.BlockSpec((nb, M2_CONV - 1, M2_CONV_DIM), lambda i: (i, 0, 0))],
        out_shape=[jax.ShapeDtypeStruct((m, M2_INNER), F32),
                   jax.ShapeDtypeStruct((m, M2_INNER, M2_STATE), F32),
                   jax.ShapeDtypeStruct((m, M2_CONV - 1, M2_CONV_DIM), F32)],
        compiler_params=_cp("parallel"),
        name="ssd_step",
    )(xbc, conv_state, dt_raw, zg, st, cw, cb, dtb, a, d, nrm)


def _merge_kernel(split5, h_ref, y5_ref, ysb_ref, ym_ref, gate_ref, w5_ref, wsb_ref, wm_ref, wo_ref, o_ref):
    tm = h_ref.shape[0]

    def branch(y, w_ref, lo):
        return _sigmoid(gate_ref[:, lo:lo + D_MODEL].astype(F32)) * jnp.dot(
            y.astype(BF16), w_ref[...], preferred_element_type=F32)

    if split5:
        y5 = jnp.concatenate([y5_ref[pl.ds(j, tm, stride=4), :] for j in range(4)], axis=1)
    else:
        y5 = y5_ref[...]
    merged = (branch(y5, w5_ref, 0) + branch(ysb_ref[...], wsb_ref, D_MODEL)
              + branch(ym_ref[...], wm_ref, 2 * D_MODEL))
    o_ref[...] = h_ref[...] + jnp.dot(merged.astype(BF16), wo_ref[...], preferred_element_type=F32)


def _merge(h, y5, ysb, ym, gates, w5, wsb, wm, wo, l):
    m = h.shape[0]
    tm = _tile(m, 512)
    row = lambda n: pl.BlockSpec((tm, n), lambda i: (i, 0))
    split5 = y5.shape[-1] == 128
    y5_spec = pl.BlockSpec((tm * 4, 128), lambda i: (i, 0)) if split5 else row(512)
    return pl.pallas_call(
        functools.partial(_merge_kernel, split5),
        grid=(m // tm,),
        in_specs=[row(D_MODEL), y5_spec, row(512), row(512), row(3 * D_MODEL),
                  _layer((512, D_MODEL), l), _layer((512, D_MODEL), l), _layer((512, D_MODEL), l),
                  _layer((D_MODEL, D_MODEL), l)],
        out_specs=row(D_MODEL),
        out_shape=jax.ShapeDtypeStruct((m, D_MODEL), F32),
        compiler_params=_cp("parallel"),
        name="merge",
    )(h, y5, ysb, ym, gates, w5, wsb, wm, wo)


def _mem_kv_kernel(x_ref, wk_ref, wv_ref, k_ref, v_ref, kb_ref, vb_ref):
    xb = x_ref[...].astype(BF16)
    k = jnp.dot(xb, wk_ref[...], preferred_element_type=F32)
    v = jnp.dot(xb, wv_ref[...], preferred_element_type=F32)
    k_ref[...] = k
    v_ref[...] = v
    kb_ref[...] = k.astype(BF16)
    vb_ref[...] = v.astype(BF16)


def _mem_kv(mem, wk, wv, l):
    m = mem.shape[0]
    tm = _tile(m, 512)
    row = pl.BlockSpec((tm, D_MODEL), lambda i: (i, 0))
    return pl.pallas_call(
        _mem_kv_kernel,
        grid=(m // tm,),
        in_specs=[row, _layer((D_MODEL, D_MODEL), l), _layer((D_MODEL, D_MODEL), l)],
        out_specs=[row] * 4,
        out_shape=[jax.ShapeDtypeStruct((m, D_MODEL), F32)] * 2 + [jax.ShapeDtypeStruct((m, D_MODEL), BF16)] * 2,
        compiler_params=_cp("parallel"),
        name="mem_kv",
    )(mem, wk, wv)


def _xattn_prompt_kernel(h_ref, g_ref, wq_ref, mk_ref, mv_ref, wo_ref, o_ref):
    h = h_ref[0]
    xn = _rms(h, g_ref[...]).astype(BF16)
    q = (jnp.dot(xn, wq_ref[...], preferred_element_type=F32) * (MEM_HEAD_DIM ** -0.5)).astype(BF16)
    heads = []
    for hd in range(MEM_HEADS):
        sl = slice(hd * MEM_HEAD_DIM, (hd + 1) * MEM_HEAD_DIM)
        s = _dot_nt(q[:, sl], mk_ref[0, :, sl])
        e = jnp.exp(s - jnp.max(s, axis=-1, keepdims=True))
        p = e / jnp.sum(e, axis=-1, keepdims=True)
        heads.append(jnp.dot(p.astype(BF16), mv_ref[0, :, sl], preferred_element_type=F32))
    o = jnp.concatenate(heads, axis=1).astype(BF16)
    o_ref[0] = h + jnp.dot(o, wo_ref[...], preferred_element_type=F32)


def _xattn_prompt(h, g, wq, mkb, mvb, wo, l):
    b, sl, _ = h.shape
    tq = _tile(sl, 512)
    mt = mkb.shape[1]
    return pl.pallas_call(
        _xattn_prompt_kernel,
        grid=(b, sl // tq),
        in_specs=[pl.BlockSpec((1, tq, D_MODEL), lambda i, j: (i, j, 0)),
                  _layer((1, D_MODEL), l), _layer((D_MODEL, D_MODEL), l),
                  pl.BlockSpec((1, mt, D_MODEL), lambda i, j: (i, 0, 0)),
                  pl.BlockSpec((1, mt, D_MODEL), lambda i, j: (i, 0, 0)),
                  _layer((D_MODEL, D_MODEL), l)],
        out_specs=pl.BlockSpec((1, tq, D_MODEL), lambda i, j: (i, j, 0)),
        out_shape=jax.ShapeDtypeStruct((b, sl, D_MODEL), F32),
        compiler_params=_cp("parallel", "parallel"),
        name="xattn_prompt",
    )(h, g, wq, mkb, mvb, wo)


def _xq_sample_kernel(h_ref, g_ref, wq_ref, q_ref):
    xn = _rms(h_ref[...], g_ref[...]).astype(BF16)
    q_ref[...] = jnp.dot(xn, wq_ref[...], preferred_element_type=F32) * (MEM_HEAD_DIM ** -0.5)


def _xattn_core_kernel(q_ref, mk_ref, mv_ref, o_ref):
    s = jnp.sum(mk_ref[...] * q_ref[...], axis=2, keepdims=True)
    e = jnp.exp(s - jnp.max(s, axis=0, keepdims=True))
    p = e / jnp.sum(e, axis=0, keepdims=True)
    o_ref[...] = jnp.sum(p * mv_ref[...], axis=0, keepdims=True)


def _xo_sample_kernel(h_ref, a_ref, wo_ref, o_ref):
    o_ref[...] = h_ref[...] + jnp.dot(a_ref[...].astype(BF16), wo_ref[...], preferred_element_type=F32)


def _xattn_sample(h, g, wq, cache_k, cache_v, wo, l):
    m = h.shape[0]
    mt = cache_k.shape[2]
    mat = _full((m, D_MODEL))
    q = pl.pallas_call(
        _xq_sample_kernel, grid=(1,),
        in_specs=[mat, _layer((1, D_MODEL), l), _layer((D_MODEL, D_MODEL), l)],
        out_specs=mat, out_shape=jax.ShapeDtypeStruct((m, D_MODEL), F32),
        compiler_params=_cp("arbitrary"), name="xattn_q",
    )(h, g, wq)
    per_seq = pl.BlockSpec((1, MEM_HEADS, MEM_HEAD_DIM), lambda i: (i, 0, 0))
    kv = pl.BlockSpec((None, None, mt, MEM_HEADS, MEM_HEAD_DIM), lambda i: (l, i, 0, 0, 0))
    att = pl.pallas_call(
        _xattn_core_kernel, grid=(m,),
        in_specs=[per_seq, kv, kv],
        out_specs=per_seq, out_shape=jax.ShapeDtypeStruct((m, MEM_HEADS, MEM_HEAD_DIM), F32),
        compiler_params=_cp("parallel"), name="xattn_sample",
    )(q.reshape(m, MEM_HEADS, MEM_HEAD_DIM), cache_k, cache_v)
    return pl.pallas_call(
        _xo_sample_kernel, grid=(1,),
        in_specs=[mat, mat, _layer((D_MODEL, D_MODEL), l)],
        out_specs=mat, out_shape=jax.ShapeDtypeStruct((m, D_MODEL), F32),
        compiler_params=_cp("arbitrary"), name="xattn_o",
    )(h, att.reshape(m, D_MODEL), wo)


def _mlp_kernel(h_ref, g_ref, w1_ref, w2_ref, *rest):
    o_ref = rest[-1]
    h = h_ref[...]
    xn = _rms(h, g_ref[...]).astype(BF16)
    acc = h
    for j in range(D_FF // 1024):
        a = jnp.maximum(jnp.dot(xn, w1_ref[:, j * 1024:(j + 1) * 1024], preferred_element_type=F32), 0.0)
        acc = acc + jnp.dot((a * a).astype(BF16), w2_ref[j * 1024:(j + 1) * 1024, :], preferred_element_type=F32)
    if len(rest) == 2:
        acc = _rms(acc, rest[0][...])
    o_ref[...] = acc


def _mlp(h, g, w1, w2, l, g_final=None):
    m = h.shape[0]
    tm = _tile(m, 512)
    row = pl.BlockSpec((tm, D_MODEL), lambda i: (i, 0))
    extra = [] if g_final is None else [g_final]
    return pl.pallas_call(
        _mlp_kernel,
        grid=(m // tm,),
        in_specs=[row, _layer((1, D_MODEL), l), _layer((D_MODEL, D_FF), l), _layer((D_FF, D_MODEL), l)]
                 + [_full((1, D_MODEL))] * len(extra),
        out_specs=row,
        out_shape=jax.ShapeDtypeStruct((m, D_MODEL), F32),
        compiler_params=_cp("parallel"),
        name="mlp",
    )(h, g, w1, w2, *extra)


def _block_diag(blocks):
    d, g, r, c = blocks.shape
    rep = (jnp.arange(c)[:, None] == jnp.arange(g * c)[None, :] % c).astype(blocks.dtype)
    wide = jnp.einsum("dik,kj->dij", blocks.reshape(d, g * r, c), rep, precision=lax.Precision.HIGHEST)
    on_diag = (jnp.arange(g * r)[:, None] // r) == (jnp.arange(g * c)[None, :] // c)
    return jnp.where(on_diag[None], wide, 0.0)


def kernel(x_prompt, x_sample, mem_prompt, cache_sb_k, cache_sb_v, state_s5_re, state_s5_im, state_ssm, state_conv, cache_mem_k, cache_mem_v, page_table, norm_mix, w_in, s5_lambda_re, s5_lambda_im, s5_log_dt, s5_b_re, s5_b_im, s5_c_re, s5_c_im, s5_d, s5_w_glu, s5_b_glu, sb_beta_bias, m2_conv_w, m2_conv_b, m2_dt_bias, m2_a_log, m2_d, m2_norm, w_br_s5, w_br_sb, w_br_m2, w_out, norm_mem, mem_wq, mem_wk, mem_wv, mem_wo, norm_mlp, mlp_w1, mlp_w2, norm_final):
    depth = w_in.shape[0]
    bsz, sl, _ = x_prompt.shape
    dec_b = x_sample.shape[0]
    mt = mem_prompt.shape[1]
    rep = lambda p: jnp.repeat(p, M2_HEAD_DIM, axis=-1)[:, None, :]
    row = lambda p: p[:, None, :]

    w_all = jnp.concatenate([w_in[:, :, :IN_MAIN], w_in[:, :, IN_GATE:],
                             jnp.repeat(w_in[:, :, IN_DT:IN_GATE], M2_HEAD_DIM, axis=-1)], axis=-1).astype(BF16)
    wglu, w5, wsb, wm, wo = (w.astype(BF16) for w in (s5_w_glu, w_br_s5, w_br_sb, w_br_m2, w_out))
    wq, wk, wv, wmo, w1, w2 = (w.astype(BF16) for w in (mem_wq, mem_wk, mem_wv, mem_wo, mlp_w1, mlp_w2))
    g_mix, g_mem, g_mlp, g_fin = row(norm_mix), row(norm_mem), row(norm_mlp), norm_final[None, :]
    s5d, bglu, cb = row(s5_d), row(s5_b_glu), row(m2_conv_b)
    dtb, a_neg, m2d, m2n = rep(m2_dt_bias), rep(-jnp.exp(m2_a_log.astype(F32))), rep(m2_d), row(m2_norm)
    tri_t = (jnp.arange(SB_TILE)[:, None] <= jnp.arange(SB_TILE)[None, :]).astype(BF16)
    bias_col = jnp.broadcast_to(sb_beta_bias[:, :, None], (depth, SB_HEADS, 128))
    h0r = state_s5_re.reshape(depth, dec_b, 2048)
    h0i = state_s5_im.reshape(depth, dec_b, 2048)
    mem2 = mem_prompt.reshape(bsz * mt, D_MODEL)

    hp = x_prompt.reshape(bsz * sl, D_MODEL)
    hs = x_sample.reshape(dec_b, D_MODEL)
    outs = [[] for _ in range(14)]
    flat = lambda p: p.reshape((depth * S5_GROUPS,) + p.shape[2:])
    wcat, poutr, pouti, pw, bbr, bbi = _s5_prep(flat(s5_lambda_re), flat(s5_lambda_im), flat(s5_log_dt),
                                                 flat(s5_b_re), flat(s5_b_im), flat(s5_c_re), flat(s5_c_im))
    power = lambda part, k: pw[:, part, k].reshape(depth, 1, 2048)
    a16r, a16i, a1r, a1i = power(0, S5_T), power(1, S5_T), power(0, 1), power(1, 1)
    per_layer = lambda p: p.reshape((depth, S5_GROUPS) + p.shape[1:])
    bbr_d = _block_diag(jnp.swapaxes(per_layer(bbr[:, :, :S5_CH]), 2, 3)).astype(BF16)
    bbi_d = _block_diag(jnp.swapaxes(per_layer(bbi[:, :, :S5_CH]), 2, 3)).astype(BF16)
    ccr_d = _block_diag(jnp.swapaxes(s5_c_re, 2, 3)).astype(BF16)
    cci_d = _block_diag(jnp.swapaxes(s5_c_im, 2, 3)).astype(BF16)

    kv_all = None
    for l in range(depth):

        u4, kt_all, vt_all, vtb, q2, k2, zg, xbc, gates, dtr = _in_proj_prompt(
            hp, g_mix, w_all, sb_beta_bias, l, bsz, kv_all)
        kv_all = (kt_all, vt_all)
        r3 = lambda t: t.reshape(bsz, sl, t.shape[-1])
        y5, s5r, s5i = _s5_prompt(u4.reshape(bsz, sl * 4, 128), wcat, poutr, pouti, a16r, a16i,
                                  s5d, wglu, bglu, l)
        ysb = _sb_prompt(r3(q2), r3(k2), vtb, tri_t)
        ym, ssm, cvs = _ssd_prompt(r3(xbc), r3(dtr), r3(zg), m2_conv_w, cb, dtb, a_neg, m2d, m2n, l)
        hp = _merge(hp, y5.reshape(bsz * sl * 4, 128), ysb.reshape(bsz * sl, SB_WIDTH),
                    ym.reshape(bsz * sl, M2_INNER), gates, w5, wsb, wm, wo, l)
        mk, mv, mkb, mvb = _mem_kv(mem2, wk, wv, l)
        hp = _xattn_prompt(hp.reshape(bsz, sl, D_MODEL), g_mem, wq, mkb.reshape(bsz, mt, D_MODEL),
                           mvb.reshape(bsz, mt, D_MODEL), wmo, l).reshape(bsz * sl, D_MODEL)
        hp = _mlp(hp, g_mlp, w1, w2, l, g_fin if l == depth - 1 else None)
        for lst, val in zip(outs[2:8], (
                s5r.reshape(bsz, S5_GROUPS, S5_STATE), s5i.reshape(bsz, S5_GROUPS, S5_STATE),
                ssm.reshape(bsz, M2_HEADS, M2_HEAD_DIM, M2_STATE), cvs,
                mk.reshape(bsz, mt, MEM_HEADS, MEM_HEAD_DIM), mv.reshape(bsz, mt, MEM_HEADS, MEM_HEAD_DIM))):
            lst.append(val)

        u, k, v, qs, zg, xbc, gates, dtr = _in_proj_sample(hs, g_mix, w_all, l)
        y5, s5r, s5i = _s5_step(u, h0r, h0i, a1r, a1i, bbr_d, bbi_d, ccr_d, cci_d, s5d, wglu, bglu, l)
        ysb = _sb_decode(qs, bias_col, cache_sb_k, cache_sb_v, page_table, l)
        ym, ssm, cvs = _ssd_step(xbc, state_conv, dtr, zg, state_ssm, m2_conv_w, cb, dtb, a_neg, m2d, m2n, l)
        hs = _merge(hs, y5, ysb, ym, gates, w5, wsb, wm, wo, l)
        hs = _xattn_sample(hs, g_mem, wq, cache_mem_k, cache_mem_v, wmo, l)
        hs = _mlp(hs, g_mlp, w1, w2, l, g_fin if l == depth - 1 else None)
        for lst, val in zip(outs[8:], (
                k.reshape(dec_b, 1, SB_HEADS, SB_HEAD_DIM), v.reshape(dec_b, 1, SB_HEADS, SB_HEAD_DIM),
                s5r.reshape(dec_b, S5_GROUPS, S5_STATE), s5i.reshape(dec_b, S5_GROUPS, S5_STATE),
                ssm.reshape(dec_b, M2_HEADS, M2_HEAD_DIM, M2_STATE), cvs)):
            lst.append(val)

    y_prompt = hp.reshape(bsz, sl, D_MODEL)
    y_sample = hs.reshape(dec_b, 1, D_MODEL)
    kv_out = [jnp.transpose(t.reshape(depth, bsz, SB_HEADS, SB_HEAD_DIM, sl), (0, 1, 4, 2, 3)) for t in kv_all]
    return (y_prompt, y_sample) + tuple(kv_out) + tuple(jnp.stack(o) for o in outs[2:])
```
